```python
import math
import jax, jax.numpy as jnp
from jax import lax
import numpy as np

D_MODEL = 2048
BATCH = 8
SEQ = 4096
DEPTH = 2

GRID_W = 64
N_BRANCH = 4
BRANCH_W = D_MODEL // N_BRANCH
NA_HEADS = 4
NA_HEAD_DIM = BRANCH_W // NA_HEADS
NA_ROWS_MAX = 8
NA_COLS = 16
DIFF_HEADS = 4
DIFF_HEAD_DIM = BRANCH_W // (2 * DIFF_HEADS)
Q_BLOCK = 128
POOL_WINDOWS = (2, 4, 8, 16)
POOL_GROUPS = len(POOL_WINDOWS)
POOL_GROUP_W = BRANCH_W // POOL_GROUPS
SGU_CHUNK = 128
SGU_GROUPS = 4
SGU_GROUP_W = BRANCH_W // SGU_GROUPS
IN_SPLITS = (3 * BRANCH_W,
             3 * BRANCH_W,
             BRANCH_W,
             2 * BRANCH_W,
             N_BRANCH * BRANCH_W)
IN_COLS = sum(IN_SPLITS)
EPS = 1e-6

kernel_name = 'hybrid_natten_diffattn_pool_sgu_encoder'


def rmsnorm(x, w):
    xf = x.astype(jnp.float32)
    y = xf * lax.rsqrt(jnp.mean(xf * xf, axis=-1, keepdims=True) + EPS)
    return (y * w.astype(jnp.float32)).astype(x.dtype)


def neighbourhood_attention(q, k, v, rel_bias):
    b, s, h, dh = q.shape
    rows = s // GRID_W
    kr = min(NA_ROWS_MAX, rows)
    qg = (q * (dh ** -0.5)).reshape(b, rows, GRID_W, h, dh)
    kg = k.reshape(b, rows, GRID_W, h, dh)
    vg = v.reshape(b, rows, GRID_W, h, dh)
    cols = jnp.arange(GRID_W)
    col_start = jnp.clip(cols - NA_COLS // 2, 0, GRID_W - NA_COLS)
    col_idx = col_start[:, None] + jnp.arange(NA_COLS)[None, :]
    col_off = col_idx - cols[:, None] + (NA_COLS - 1)

    def one_row(r):
        rs = jnp.clip(r - kr // 2, 0, rows - kr)
        k_rows = lax.dynamic_slice_in_dim(kg, rs, kr, axis=1)
        v_rows = lax.dynamic_slice_in_dim(vg, rs, kr, axis=1)
        k_sel = k_rows[:, :, col_idx]
        v_sel = v_rows[:, :, col_idx]
        q_row = lax.dynamic_index_in_dim(qg, r, axis=1, keepdims=False)
        row_off = rs + jnp.arange(kr) - r + (NA_ROWS_MAX - 1)
        bias = rel_bias[:, row_off[:, None, None], col_off[None, :, :]]
        bias = bias.transpose(0, 2, 1, 3).astype(jnp.float32)
        scores = jnp.einsum('bchd,brcjhd->bhcrj', q_row, k_sel).astype(jnp.float32) + bias[None]
        p = jax.nn.softmax(scores.reshape(b, h, GRID_W, kr * NA_COLS), axis=-1)
        p = p.reshape(b, h, GRID_W, kr, NA_COLS).astype(v.dtype)
        return jnp.einsum('bhcrj,brcjhd->bchd', p, v_sel)

    out = lax.map(one_row, jnp.arange(rows))
    return out.transpose(1, 0, 2, 3, 4).reshape(b, s, h * dh)


def diff_attention(q, k, v, lam, subln_w, lambda_init):
    b, s, h, _, dk = q.shape
    nq = s // Q_BLOCK
    slopes = 2.0 ** (-8.0 * jnp.arange(1, h + 1, dtype=jnp.float32) / h)
    qb = (q * (dk ** -0.5)).reshape(b, nq, Q_BLOCK, h, 2, dk).transpose(1, 0, 2, 3, 4, 5)
    kpos = jnp.arange(s)

    def one_block(args):
        qblk, i = args
        qpos = i * Q_BLOCK + jnp.arange(Q_BLOCK)
        dist = jnp.abs(qpos[:, None] - kpos[None, :]).astype(jnp.float32)
        alibi = -slopes[:, None, None] * dist[None]
        scores = jnp.einsum('bqhcd,bkhcd->bhcqk', qblk, k).astype(jnp.float32) + alibi[None, :, None]
        p = jax.nn.softmax(scores, axis=-1)
        a = (p[:, :, 0] - lam * p[:, :, 1]).astype(v.dtype)
        return jnp.einsum('bhqk,bkhd->bqhd', a, v)

    out = lax.map(one_block, (qb, jnp.arange(nq)))
    out = out.transpose(1, 0, 2, 3, 4).reshape(b, s, h, 2 * dk)
    out = rmsnorm(out, subln_w) * (1.0 - lambda_init)
    return out.reshape(b, s, h * 2 * dk)


def multiscale_pool(x, w_groups, scale):
    b, s, c = x.shape
    xf = x.astype(jnp.float32)
    csum = jnp.concatenate([jnp.zeros((b, 1, c), jnp.float32), jnp.cumsum(xf, axis=1)], axis=1)
    t = jnp.arange(s)
    outs = []
    for g, win in enumerate(POOL_WINDOWS):
        lo = jnp.clip(t - win // 2, 0, s)
        hi = jnp.clip(t + win - win // 2, 0, s)
        sl = slice(g * POOL_GROUP_W, (g + 1) * POOL_GROUP_W)
        cg = csum[:, :, sl]
        mean = (cg[:, hi] - cg[:, lo]) / (hi - lo).astype(jnp.float32)[None, :, None]
        pooled = (mean - xf[:, :, sl]).astype(x.dtype)
        outs.append(pooled @ w_groups[g])
    return jnp.concatenate(outs, axis=-1) * scale


def spatial_gating(u, v, ln_w, ln_b, w_s, b_s):
    b, s, c = v.shape
    vf = v.astype(jnp.float32)
    mu = jnp.mean(vf, axis=-1, keepdims=True)
    var = jnp.mean(jnp.square(vf - mu), axis=-1, keepdims=True)
    vn = ((vf - mu) * lax.rsqrt(var + EPS) * ln_w.astype(jnp.float32) + ln_b.astype(jnp.float32)).astype(v.dtype)
    vc = vn.reshape(b, s // SGU_CHUNK, SGU_CHUNK, SGU_GROUPS, SGU_GROUP_W)
    mixed = jnp.einsum('gpq,bnqgc->bnpgc', w_s, vc) + b_s.T[None, None, :, :, None]
    return u * mixed.reshape(b, s, c)


def setup_inputs(seed: int = 0) -> dict:
    key = jax.random.key(seed)
    ks = jax.random.split(key, 20)
    f32 = jnp.float32
    nrm = lambda k, shape, scale: jax.random.normal(k, shape, f32) * scale
    x = jax.random.normal(ks[0], (BATCH, SEQ, D_MODEL), f32)
    return {
        'x': x,
        'norm_w': 1.0 + nrm(ks[1], (DEPTH, D_MODEL), 0.02),
        'w_in': nrm(ks[2], (DEPTH, D_MODEL, IN_COLS), D_MODEL ** -0.5),
        'na_bias': nrm(ks[3], (DEPTH, NA_HEADS, 2 * NA_ROWS_MAX - 1, 2 * NA_COLS - 1), 0.1),
        'lam_q1': nrm(ks[4], (DEPTH, DIFF_HEAD_DIM), 0.1),
        'lam_k1': nrm(ks[5], (DEPTH, DIFF_HEAD_DIM), 0.1),
        'lam_q2': nrm(ks[6], (DEPTH, DIFF_HEAD_DIM), 0.1),
        'lam_k2': nrm(ks[7], (DEPTH, DIFF_HEAD_DIM), 0.1),
        'diff_subln_w': 1.0 + nrm(ks[8], (DEPTH, 2 * DIFF_HEAD_DIM), 0.02),
        'pool_w': nrm(ks[9], (DEPTH, POOL_GROUPS, POOL_GROUP_W, POOL_GROUP_W), POOL_GROUP_W ** -0.5),
        'pool_scale': 1.0 + nrm(ks[10], (DEPTH, BRANCH_W), 0.02),
        'sgu_ln_w': 1.0 + nrm(ks[11], (DEPTH, BRANCH_W), 0.02),
        'sgu_ln_b': nrm(ks[12], (DEPTH, BRANCH_W), 0.02),
        'sgu_ws': nrm(ks[13], (DEPTH, SGU_GROUPS, SGU_CHUNK, SGU_CHUNK), SGU_CHUNK ** -0.5),
        'sgu_b': 1.0 + nrm(ks[14], (DEPTH, SGU_GROUPS, SGU_CHUNK), 0.02),
        'w_branch': nrm(ks[15], (DEPTH, N_BRANCH, BRANCH_W, D_MODEL), BRANCH_W ** -0.5),
        'w_gate': nrm(ks[16], (DEPTH, N_BRANCH, D_MODEL, D_MODEL), D_MODEL ** -0.5),
        'w_out': nrm(ks[17], (DEPTH, D_MODEL, D_MODEL), D_MODEL ** -0.5),
        'final_norm_w': 1.0 + nrm(ks[18], (D_MODEL,), 0.02),
    }


def reference(x, norm_w, w_in, na_bias, lam_q1, lam_k1, lam_q2, lam_k2, diff_subln_w,
              pool_w, pool_scale, sgu_ln_w, sgu_ln_b, sgu_ws, sgu_b,
              w_branch, w_gate, w_out, final_norm_w):
    b, s, _ = x.shape
    split_pts = [int(p) for p in np.cumsum(IN_SPLITS)[:-1]]
    for l in range(DEPTH):
        h = rmsnorm(x, norm_w[l])
        proj = h @ w_in[l]
        na_in, diff_in, pool_in, sgu_in, z = jnp.split(proj, split_pts, axis=-1)

        qa, ka, va = [t.reshape(b, s, NA_HEADS, NA_HEAD_DIM) for t in jnp.split(na_in, 3, axis=-1)]
        y_a = neighbourhood_attention(qa, ka, va, na_bias[l])

        qd, kd, vd = jnp.split(diff_in, 3, axis=-1)
        qd = qd.reshape(b, s, DIFF_HEADS, 2, DIFF_HEAD_DIM)
        kd = kd.reshape(b, s, DIFF_HEADS, 2, DIFF_HEAD_DIM)
        vd = vd.reshape(b, s, DIFF_HEADS, 2 * DIFF_HEAD_DIM)
        lambda_init = 0.8 - 0.6 * math.exp(-0.3 * l)
        lam = (jnp.exp(jnp.sum(lam_q1[l].astype(jnp.float32) * lam_k1[l].astype(jnp.float32)))
               - jnp.exp(jnp.sum(lam_q2[l].astype(jnp.float32) * lam_k2[l].astype(jnp.float32)))
               + lambda_init)
        y_b = diff_attention(qd, kd, vd, lam, diff_subln_w[l], lambda_init)

        y_c = multiscale_pool(pool_in, pool_w[l], pool_scale[l])

        u, v = jnp.split(sgu_in, 2, axis=-1)
        y_d = spatial_gating(u, v, sgu_ln_w[l], sgu_ln_b[l], sgu_ws[l], sgu_b[l])

        z = z.reshape(b, s, N_BRANCH, BRANCH_W)
        merged = jnp.zeros_like(x)
        for i, y in enumerate((y_a, y_b, y_c, y_d)):
            y = y * jax.nn.silu(z[:, :, i])
            merged = merged + jax.nn.sigmoid(h @ w_gate[l, i]) * (y @ w_branch[l, i])
        x = x + merged @ w_out[l]
    return rmsnorm(x, final_norm_w)
```

```python
import functools
import math

import jax
import jax.numpy as jnp
from jax import lax
from jax.experimental import pallas as pl
from jax.experimental.pallas import tpu as pltpu

D_MODEL = 2048
GRID_W = 64
N_BRANCH = 4
BRANCH_W = D_MODEL // N_BRANCH
NA_HEADS = 4
NA_HEAD_DIM = BRANCH_W // NA_HEADS
NA_ROWS_MAX = 8
NA_COLS = 16
DIFF_HEADS = 4
DIFF_HEAD_DIM = BRANCH_W // (2 * DIFF_HEADS)
POOL_WINDOWS = (2, 4, 8, 16)
POOL_GROUP_W = BRANCH_W // len(POOL_WINDOWS)
SGU_CHUNK = 128
SGU_GROUPS = 4
SGU_GROUP_W = BRANCH_W // SGU_GROUPS
IN_COLS = 13 * BRANCH_W
EPS = 1e-6

COL_NA_Q = 0
COL_NA_K = BRANCH_W
COL_NA_V = 2 * BRANCH_W
COL_DF_Q = 3 * BRANCH_W
COL_DF_K = 4 * BRANCH_W
COL_DF_V = 5 * BRANCH_W
COL_POOL = 6 * BRANCH_W
COL_SGU_U = 7 * BRANCH_W
COL_SGU_V = 8 * BRANCH_W
COL_Z = 9 * BRANCH_W

LANES = 128
VMEM_LIMIT = 56 * 1024 * 1024
MASK_VALUE = -1e30

BF16 = jnp.bfloat16
F32 = jnp.float32


def _params(sem, vmem=VMEM_LIMIT):
    return pltpu.CompilerParams(dimension_semantics=sem, vmem_limit_bytes=vmem)


def _silu(z):
    return z * (1.0 / (1.0 + jnp.exp(-z)))


def _sigmoid(z):
    return 1.0 / (1.0 + jnp.exp(-z))


def _inproj_kernel(x_ref, nw_ref, w_ref, cs_ref, h_ref, p_ref):
    @pl.when(pl.program_id(1) == 0)
    def _():
        x = x_ref[...]
        ms = jnp.mean(x * x, axis=-1, keepdims=True)
        h_ref[...] = (x * lax.rsqrt(ms + EPS) * nw_ref[...]).astype(BF16)

    acc = jnp.dot(h_ref[...], w_ref[...], preferred_element_type=F32)
    p_ref[...] = (acc * cs_ref[...]).astype(BF16)


def _inproj(x2, norm_w, w_in_bf, col_scale, tm=1024, tn=512):
    t = x2.shape[0]
    return pl.pallas_call(
        _inproj_kernel,
        grid=(t // tm, IN_COLS // tn),
        in_specs=[
            pl.BlockSpec((tm, D_MODEL), lambda i, j: (i, 0)),
            pl.BlockSpec((1, D_MODEL), lambda i, j: (0, 0)),
            pl.BlockSpec((D_MODEL, tn), lambda i, j: (0, j)),
            pl.BlockSpec((1, tn), lambda i, j: (0, j)),
        ],
        out_specs=[
            pl.BlockSpec((tm, D_MODEL), lambda i, j: (i, 0)),
            pl.BlockSpec((tm, tn), lambda i, j: (i, j)),
        ],
        out_shape=[
            jax.ShapeDtypeStruct((t, D_MODEL), BF16),
            jax.ShapeDtypeStruct((t, IN_COLS), BF16),
        ],
        compiler_params=_params(("parallel", "arbitrary")),
        name="inproj",
    )(x2, norm_w.reshape(1, D_MODEL), w_in_bf, col_scale)


NA_QROWS = 8
NA_KROWS = 16
NA_TQ = NA_QROWS * GRID_W
NA_TK = NA_KROWS * GRID_W
NA_KSPLIT = 4
NA_KBLK = NA_TK // NA_KSPLIT


def _na_window_start(rb, rows):
    lo = rb * NA_QROWS - NA_ROWS_MAX // 2
    return jnp.clip(lo, 0, rows - NA_KROWS)


def _na_dense_bias(rel_bias, rows):
    nrb = rows // NA_QROWS
    kr_win = min(NA_ROWS_MAX, rows)
    rb = jnp.arange(nrb)
    r = rb[:, None] * NA_QROWS + jnp.arange(NA_QROWS)[None, :]
    start = _na_window_start(rb, rows)
    kr = start[:, None] + jnp.arange(NA_KROWS)[None, :]
    rs = jnp.clip(r - kr_win // 2, 0, rows - kr_win)
    row_ok = (kr[:, None, :] >= rs[:, :, None]) & (kr[:, None, :] < rs[:, :, None] + kr_win)
    row_off = jnp.clip(kr[:, None, :] - r[:, :, None] + (NA_ROWS_MAX - 1), 0, 2 * NA_ROWS_MAX - 2)
    c = jnp.arange(GRID_W)
    cs = jnp.clip(c - NA_COLS // 2, 0, GRID_W - NA_COLS)
    col_ok = (c[None, :] >= cs[:, None]) & (c[None, :] < cs[:, None] + NA_COLS)
    col_off = jnp.clip(c[None, :] - c[:, None] + (NA_COLS - 1), 0, 2 * NA_COLS - 2)
    vals = rel_bias[:, row_off[:, :, :, None, None], col_off[None, None, None, :, :]]
    ok = row_ok[:, :, :, None, None] & col_ok[None, None, None, :, :]
    dense = jnp.where(ok[None], vals.astype(F32), MASK_VALUE)
    dense = dense.transpose(1, 0, 2, 4, 3, 5)
    return dense.reshape(nrb, NA_HEADS, NA_TQ, NA_TK)


def _na_kernel(q_ref, k0, k1, k2, k3, v0, v1, v2, v3, z_ref, b_ref, o_ref):
    k_refs = (k0, k1, k2, k3)
    v_refs = (v0, v1, v2, v3)
    nt = (((1,), (1,)), ((), ()))
    for h in range(NA_HEADS):
        hs = slice(h * NA_HEAD_DIM, (h + 1) * NA_HEAD_DIM)
        q = q_ref[:, hs]
        s = jnp.concatenate(
            [lax.dot_general(q, kr[:, hs], nt, preferred_element_type=F32) for kr in k_refs],
            axis=-1)
        s = s + b_ref[h]
        m = jnp.max(s, axis=-1, keepdims=True)
        p = jnp.exp(s - m)
        l = jnp.sum(p, axis=-1, keepdims=True)
        pb = p.astype(BF16)
        o = jnp.zeros((NA_TQ, NA_HEAD_DIM), F32)
        for j, vr in enumerate(v_refs):
            o = o + jnp.dot(pb[:, j * NA_KBLK:(j + 1) * NA_KBLK], vr[:, hs],
                            preferred_element_type=F32)
        o = o / l
        o_ref[:, hs] = (o * _silu(z_ref[:, hs].astype(F32))).astype(BF16)


def _na_attention(proj, dense_bias, batch, seq):
    rows = seq // GRID_W
    nrb = rows // NA_QROWS
    t = batch * seq
    qblk_per_seq = seq // NA_TQ
    kblk_per_seq = seq // NA_KBLK
    kblk_per_row = NA_KBLK // GRID_W

    def q_map(col):
        return lambda rb, b: (b * qblk_per_seq + rb, col)

    def kv_map(col, j):
        def f(rb, b):
            start = _na_window_start(rb, rows) // kblk_per_row
            return (b * kblk_per_seq + start + j, col)
        return f

    in_specs = [pl.BlockSpec((NA_TQ, BRANCH_W), q_map(COL_NA_Q // BRANCH_W))]
    in_specs += [pl.BlockSpec((NA_KBLK, BRANCH_W), kv_map(COL_NA_K // BRANCH_W, j)) for j in range(NA_KSPLIT)]
    in_specs += [pl.BlockSpec((NA_KBLK, BRANCH_W), kv_map(COL_NA_V // BRANCH_W, j)) for j in range(NA_KSPLIT)]
    in_specs += [pl.BlockSpec((NA_TQ, BRANCH_W), q_map(COL_Z // BRANCH_W + 0))]
    in_specs += [pl.BlockSpec((None, NA_HEADS, NA_TQ, NA_TK), lambda rb, b: (rb, 0, 0, 0))]
    return pl.pallas_call(
        _na_kernel,
        grid=(nrb, batch),
        in_specs=in_specs,
        out_specs=pl.BlockSpec((NA_TQ, BRANCH_W), lambda rb, b: (b * qblk_per_seq + rb, 0)),
        out_shape=jax.ShapeDtypeStruct((t, BRANCH_W), BF16),
        compiler_params=_params(("parallel", "parallel")),
        name="na_attention",
    )(*([proj] * 10), dense_bias)


def _diff_kernel(lam_ref, sw_ref, q_ref, k_ref, v_ref, z_ref, o_ref, m_ref, l_ref, acc_ref,
                 *, tq, tk, lambda_init):
    h = pl.program_id(1)
    qi = pl.program_id(2)
    kj = pl.program_id(3)
    nt = (((1,), (1,)), ((), ()))

    @pl.when(kj == 0)
    def _():
        m_ref[...] = jnp.full(m_ref.shape, MASK_VALUE, F32)
        l_ref[...] = jnp.zeros(l_ref.shape, F32)
        acc_ref[...] = jnp.zeros(acc_ref.shape, F32)

    slope = jnp.where(h == 0, 0.25, jnp.where(h == 1, 0.0625, jnp.where(h == 2, 0.015625, 0.00390625)))
    qpos = qi * tq + lax.broadcasted_iota(jnp.int32, (tq, tk), 0)
    kpos = kj * tk + lax.broadcasted_iota(jnp.int32, (tq, tk), 1)
    bias = jnp.abs(qpos - kpos).astype(F32) * (-slope)

    q = q_ref[...]
    k = k_ref[...]
    v = v_ref[...]
    lane = lax.broadcasted_iota(jnp.int32, q.shape, 1)
    for c in range(2):
        in_c = (lane >= c * DIFF_HEAD_DIM) & (lane < (c + 1) * DIFF_HEAD_DIM)
        qc = jnp.where(in_c, q, jnp.zeros_like(q))
        s = lax.dot_general(qc, k, nt, preferred_element_type=F32) + bias
        m_prev = m_ref[c]
        m_new = jnp.maximum(m_prev, jnp.max(s, axis=-1, keepdims=True))
        alpha = jnp.exp(m_prev - m_new)
        p = jnp.exp(s - m_new[:, :1])
        l_ref[c] = alpha * l_ref[c] + jnp.sum(p, axis=-1, keepdims=True)
        acc_ref[c] = alpha * acc_ref[c] + jnp.dot(p.astype(BF16), v, preferred_element_type=F32)
        m_ref[c] = m_new

    @pl.when(kj == pl.num_programs(3) - 1)
    def _():
        lv = lam_ref[...]
        lam = (jnp.exp(jnp.sum(lv[0:1] * lv[1:2], axis=-1, keepdims=True))
               - jnp.exp(jnp.sum(lv[2:3] * lv[3:4], axis=-1, keepdims=True)) + lambda_init)
        a = acc_ref[0] / l_ref[0] - lam * (acc_ref[1] / l_ref[1])
        ms = jnp.mean(a * a, axis=-1, keepdims=True)
        y = a * lax.rsqrt(ms + EPS) * sw_ref[...] * (1.0 - lambda_init)
        o_ref[...] = (y * _silu(z_ref[...].astype(F32))).astype(BF16)


def _diff_attention(proj, lam_vecs, subln_w, batch, seq, lambda_init, tq=512, tk=1024):
    t = batch * seq
    nq = seq // tq
    nk = seq // tk
    hw = 2 * DIFF_HEAD_DIM
    kern = functools.partial(_diff_kernel, tq=tq, tk=tk, lambda_init=lambda_init)
    return pl.pallas_call(
        kern,
        grid=(batch, DIFF_HEADS, nq, nk),
        in_specs=[
            pl.BlockSpec((4, DIFF_HEAD_DIM), lambda b, h, i, j: (0, 0)),
            pl.BlockSpec((1, hw), lambda b, h, i, j: (0, 0)),
            pl.BlockSpec((tq, hw), lambda b, h, i, j: (b * nq + i, COL_DF_Q // hw + h)),
            pl.BlockSpec((tk, hw), lambda b, h, i, j: (b * nk + j, COL_DF_K // hw + h)),
            pl.BlockSpec((tk, hw), lambda b, h, i, j: (b * nk + j, COL_DF_V // hw + h)),
            pl.BlockSpec((tq, hw), lambda b, h, i, j: (b * nq + i, (COL_Z + BRANCH_W) // hw + h)),
        ],
        out_specs=pl.BlockSpec((tq, hw), lambda b, h, i, j: (b * nq + i, h)),
        out_shape=jax.ShapeDtypeStruct((t, BRANCH_W), BF16),
        scratch_shapes=[
            pltpu.VMEM((2, tq, hw), F32),
            pltpu.VMEM((2, tq, hw), F32),
            pltpu.VMEM((2, tq, hw), F32),
        ],
        compiler_params=_params(("parallel", "parallel", "parallel", "arbitrary")),
        name="diff_attention",
    )(lam_vecs, subln_w.reshape(1, hw), proj, proj, proj, proj)


def _pool_kernel(x_ref, z_ref, w_ref, sc_ref, o_ref, *, seq):
    t = lax.broadcasted_iota(jnp.int32, (seq, POOL_GROUP_W), 0)
    for g, win in enumerate(POOL_WINDOWS):
        gs = slice(g * POOL_GROUP_W, (g + 1) * POOL_GROUP_W)
        x = x_ref[:, gs].astype(F32)
        half = win // 2
        total = x
        for d in range(-half, win - half):
            if d == 0:
                continue
            shifted = pltpu.roll(x, (-d) % seq, axis=0)
            ok = (t + d >= 0) & (t + d < seq)
            total = total + jnp.where(ok, shifted, 0.0)
        cnt = jnp.minimum(t + (win - half), seq) - jnp.maximum(t - half, 0)
        pooled = total / cnt.astype(F32) - x
        y = jnp.dot(pooled.astype(BF16), w_ref[g], preferred_element_type=F32) * sc_ref[:, gs]
        o_ref[:, gs] = (y * _silu(z_ref[:, gs].astype(F32))).astype(BF16)


def _pool_mixer(proj, pool_w_bf, pool_scale, batch, seq):
    t = batch * seq
    return pl.pallas_call(
        functools.partial(_pool_kernel, seq=seq),
        grid=(batch,),
        in_specs=[
            pl.BlockSpec((seq, BRANCH_W), lambda b: (b, COL_POOL // BRANCH_W)),
            pl.BlockSpec((seq, BRANCH_W), lambda b: (b, COL_Z // BRANCH_W + 2)),
            pl.BlockSpec((len(POOL_WINDOWS), POOL_GROUP_W, POOL_GROUP_W), lambda b: (0, 0, 0)),
            pl.BlockSpec((1, BRANCH_W), lambda b: (0, 0)),
        ],
        out_specs=pl.BlockSpec((seq, BRANCH_W), lambda b: (b, 0)),
        out_shape=jax.ShapeDtypeStruct((t, BRANCH_W), BF16),
        compiler_params=_params(("parallel",)),
        name="pool_mixer",
    )(proj, proj, pool_w_bf, pool_scale.reshape(1, BRANCH_W))


def _sgu_kernel(u_ref, v_ref, z_ref, lw_ref, lb_ref, ws_ref, bs_ref, o_ref, *, tm):
    v = v_ref[...].astype(F32)
    mu = jnp.mean(v, axis=-1, keepdims=True)
    vc = v - mu
    var = jnp.mean(vc * vc, axis=-1, keepdims=True)
    vn = (vc * lax.rsqrt(var + EPS) * lw_ref[...] + lb_ref[...]).astype(BF16)
    for n in range(tm // SGU_CHUNK):
        rs = slice(n * SGU_CHUNK, (n + 1) * SGU_CHUNK)
        for g in range(SGU_GROUPS):
            gs = slice(g * SGU_GROUP_W, (g + 1) * SGU_GROUP_W)
            mixed = jnp.dot(ws_ref[g], vn[rs, gs], preferred_element_type=F32) + bs_ref[g]
            y = u_ref[rs, gs].astype(F32) * mixed
            o_ref[rs, gs] = (y * _silu(z_ref[rs, gs].astype(F32))).astype(BF16)


def _sgu_mixer(proj, ln_w, ln_b, ws_bf, bs_full, t, tm=512):
    return pl.pallas_call(
        functools.partial(_sgu_kernel, tm=tm),
        grid=(t // tm,),
        in_specs=[
            pl.BlockSpec((tm, BRANCH_W), lambda i: (i, COL_SGU_U // BRANCH_W)),
            pl.BlockSpec((tm, BRANCH_W), lambda i: (i, COL_SGU_V // BRANCH_W)),
            pl.BlockSpec((tm, BRANCH_W), lambda i: (i, COL_Z // BRANCH_W + 3)),
            pl.BlockSpec((1, BRANCH_W), lambda i: (0, 0)),
            pl.BlockSpec((1, BRANCH_W), lambda i: (0, 0)),
            pl.BlockSpec((SGU_GROUPS, SGU_CHUNK, SGU_CHUNK), lambda i: (0, 0, 0)),
            pl.BlockSpec((SGU_GROUPS, SGU_CHUNK, SGU_GROUP_W), lambda i: (0, 0, 0)),
        ],
        out_specs=pl.BlockSpec((tm, BRANCH_W), lambda i: (i, 0)),
        out_shape=jax.ShapeDtypeStruct((t, BRANCH_W), BF16),
        compiler_params=_params(("parallel",)),
        name="sgu_mixer",
    )(proj, proj, proj, ln_w.reshape(1, BRANCH_W), ln_b.reshape(1, BRANCH_W), ws_bf, bs_full)


def _merge_kernel(h_ref, ya_ref, yb_ref, yc_ref, yd_ref, wg_ref, wb_ref, o_ref):
    h = h_ref[...]
    merged = None
    for i, y_ref in enumerate((ya_ref, yb_ref, yc_ref, yd_ref)):
        gate = _sigmoid(jnp.dot(h, wg_ref[i], preferred_element_type=F32))
        term = gate * jnp.dot(y_ref[...], wb_ref[i], preferred_element_type=F32)
        merged = term if merged is None else merged + term
    o_ref[...] = merged.astype(BF16)


def _merge(h, ys, wg_bf, wb_bf, tm=1024, tn=512):
    t = h.shape[0]
    y_spec = pl.BlockSpec((tm, BRANCH_W), lambda i, j: (i, 0))
    return pl.pallas_call(
        _merge_kernel,
        grid=(t // tm, D_MODEL // tn),
        in_specs=[
            pl.BlockSpec((tm, D_MODEL), lambda i, j: (i, 0)),
            y_spec, y_spec, y_spec, y_spec,
            pl.BlockSpec((N_BRANCH, D_MODEL, tn), lambda i, j: (0, 0, j)),
            pl.BlockSpec((N_BRANCH, BRANCH_W, tn), lambda i, j: (0, 0, j)),
        ],
        out_specs=pl.BlockSpec((tm, tn), lambda i, j: (i, j)),
        out_shape=jax.ShapeDtypeStruct((t, D_MODEL), BF16),
        compiler_params=_params(("parallel", "arbitrary")),
        name="merge",
    )(h, *ys, wg_bf, wb_bf)


def _outproj_kernel(x_ref, m_ref, w_ref, fw_ref, o_ref, *, final_norm):
    x = x_ref[...] + jnp.dot(m_ref[...], w_ref[...], preferred_element_type=F32)
    if final_norm:
        ms = jnp.mean(x * x, axis=-1, keepdims=True)
        x = x * lax.rsqrt(ms + EPS) * fw_ref[...]
    o_ref[...] = x


def _outproj(x2, merged, w_out_bf, final_w, final_norm, tm=512):
    t = x2.shape[0]
    return pl.pallas_call(
        functools.partial(_outproj_kernel, final_norm=final_norm),
        grid=(t // tm,),
        in_specs=[
            pl.BlockSpec((tm, D_MODEL), lambda i: (i, 0)),
            pl.BlockSpec((tm, D_MODEL), lambda i: (i, 0)),
            pl.BlockSpec((D_MODEL, D_MODEL), lambda i: (0, 0)),
            pl.BlockSpec((1, D_MODEL), lambda i: (0, 0)),
        ],
        out_specs=pl.BlockSpec((tm, D_MODEL), lambda i: (i, 0)),
        out_shape=jax.ShapeDtypeStruct((t, D_MODEL), F32),
        compiler_params=_params(("parallel",)),
        name="outproj",
    )(x2, merged, w_out_bf, final_w.reshape(1, D_MODEL))


def _column_scale():
    cs = jnp.ones((IN_COLS,), F32)
    cs = cs.at[COL_NA_Q:COL_NA_Q + BRANCH_W].set(NA_HEAD_DIM ** -0.5)
    cs = cs.at[COL_DF_Q:COL_DF_Q + BRANCH_W].set(DIFF_HEAD_DIM ** -0.5)
    return cs.reshape(1, IN_COLS)


def kernel(x, norm_w, w_in, na_bias, lam_q1, lam_k1, lam_q2, lam_k2, diff_subln_w, pool_w, pool_scale,
           sgu_ln_w, sgu_ln_b, sgu_ws, sgu_b, w_branch, w_gate, w_out, final_norm_w):
    batch, seq, _ = x.shape
    depth = norm_w.shape[0]
    t = batch * seq
    rows = seq // GRID_W
    x2 = x.reshape(t, D_MODEL)
    col_scale = _column_scale()
    for l in range(depth):
        h, proj = _inproj(x2, norm_w[l], w_in[l].astype(BF16), col_scale)
        y_a = _na_attention(proj, _na_dense_bias(na_bias[l], rows), batch, seq)
        lambda_init = 0.8 - 0.6 * math.exp(-0.3 * l)
        lam_vecs = jnp.stack([lam_q1[l], lam_k1[l], lam_q2[l], lam_k2[l]]).astype(F32)
        y_b = _diff_attention(proj, lam_vecs, diff_subln_w[l], batch, seq, lambda_init)
        y_c = _pool_mixer(proj, pool_w[l].astype(BF16), pool_scale[l], batch, seq)
        bs_full = jnp.broadcast_to(sgu_b[l][:, :, None], (SGU_GROUPS, SGU_CHUNK, SGU_GROUP_W))
        y_d = _sgu_mixer(proj, sgu_ln_w[l], sgu_ln_b[l], sgu_ws[l].astype(BF16), bs_full, t)
        merged = _merge(h, (y_a, y_b, y_c, y_d), w_gate[l].astype(BF16), w_branch[l].astype(BF16))
        x2 = _outproj(x2, merged, w_out[l].astype(BF16), final_norm_w, l == depth - 1)
    return x2.reshape(batch, seq, D_MODEL)
```

```python
import functools
import math

import jax
import jax.numpy as jnp
import numpy as np
from jax import lax
from jax.experimental import pallas as pl
from jax.experimental.pallas import tpu as pltpu

D_MODEL = 2048
GRID_W = 64
N_BRANCH = 4
BRANCH_W = D_MODEL // N_BRANCH
NA_HEADS = 4
NA_HEAD_DIM = BRANCH_W // NA_HEADS
NA_ROWS_MAX = 8
NA_COLS = 16
DIFF_HEADS = 4
DIFF_HEAD_DIM = BRANCH_W // (2 * DIFF_HEADS)
POOL_WINDOWS = (2, 4, 8, 16)
POOL_GROUP_W = BRANCH_W // len(POOL_WINDOWS)
SGU_CHUNK = 128
SGU_GROUPS = 4
SGU_GROUP_W = BRANCH_W // SGU_GROUPS
IN_COLS = 13 * BRANCH_W
EPS = 1e-6

COL_NA_Q = 0
COL_NA_K = BRANCH_W
COL_NA_V = 2 * BRANCH_W
COL_DF_Q = 3 * BRANCH_W
COL_DF_K = 4 * BRANCH_W
COL_DF_V = 5 * BRANCH_W
COL_POOL = 6 * BRANCH_W
COL_SGU_U = 7 * BRANCH_W
COL_SGU_V = 8 * BRANCH_W
COL_Z = 9 * BRANCH_W

LANES = 128
VMEM_LIMIT = 56 * 1024 * 1024
MASK_VALUE = -1e30

BF16 = jnp.bfloat16
F32 = jnp.float32


def _params(sem, vmem=VMEM_LIMIT):
    return pltpu.CompilerParams(dimension_semantics=sem, vmem_limit_bytes=vmem)


def _silu(z):
    return z * (1.0 / (1.0 + jnp.exp(-z)))


def _sigmoid(z):
    return 1.0 / (1.0 + jnp.exp(-z))


def _inproj_kernel(x_ref, nw_ref, w_ref, cs_ref, h_ref, p_ref):
    @pl.when(pl.program_id(1) == 0)
    def _():
        x = x_ref[...]
        ms = jnp.mean(x * x, axis=-1, keepdims=True)
        h_ref[...] = (x * lax.rsqrt(ms + EPS) * nw_ref[...]).astype(BF16)

    acc = jnp.dot(h_ref[...], w_ref[...], preferred_element_type=F32)
    p_ref[...] = (acc * cs_ref[...]).astype(BF16)


def _inproj(x2, norm_w, w_in_bf, col_scale, tm=1024, tn=512):
    t = x2.shape[0]
    return pl.pallas_call(
        _inproj_kernel,
        grid=(t // tm, IN_COLS // tn),
        in_specs=[
            pl.BlockSpec((tm, D_MODEL), lambda i, j: (i, 0)),
            pl.BlockSpec((1, D_MODEL), lambda i, j: (0, 0)),
            pl.BlockSpec((D_MODEL, tn), lambda i, j: (0, j)),
            pl.BlockSpec((1, tn), lambda i, j: (0, j)),
        ],
        out_specs=[
            pl.BlockSpec((tm, D_MODEL), lambda i, j: (i, 0)),
            pl.BlockSpec((tm, tn), lambda i, j: (i, j)),
        ],
        out_shape=[
            jax.ShapeDtypeStruct((t, D_MODEL), BF16),
            jax.ShapeDtypeStruct((t, IN_COLS), BF16),
        ],
        compiler_params=_params(("parallel", "arbitrary")),
        name="inproj",
    )(x2, norm_w.reshape(1, D_MODEL), w_in_bf, col_scale)


NA_QROWS = 8
NA_KROWS = 16
NA_TQ = NA_QROWS * GRID_W
NA_TK = NA_KROWS * GRID_W
NA_KSPLIT = 4
NA_KBLK = NA_TK // NA_KSPLIT


def _na_window_start(rb, rows):
    lo = rb * NA_QROWS - NA_ROWS_MAX // 2
    return jnp.clip(lo, 0, rows - NA_KROWS)


def _na_patterns(rows):
    nrb = rows // NA_QROWS
    kr_win = min(NA_ROWS_MAX, rows)
    n_roff = 2 * NA_ROWS_MAX - 1
    n_coff = 2 * NA_COLS - 1
    pats, pat_ids = [], []
    for rb in range(nrb):
        r = rb * NA_QROWS + np.arange(NA_QROWS)
        start = int(np.clip(rb * NA_QROWS - NA_ROWS_MAX // 2, 0, rows - NA_KROWS))
        kr = start + np.arange(NA_KROWS)
        rs = np.clip(r - kr_win // 2, 0, rows - kr_win)
        ok = (kr[None, :] >= rs[:, None]) & (kr[None, :] < rs[:, None] + kr_win)
        off = kr[None, :] - r[:, None] + (NA_ROWS_MAX - 1)
        sel = np.zeros((NA_QROWS, NA_KROWS, n_roff), np.float32)
        for i in range(NA_QROWS):
            for j in range(NA_KROWS):
                if ok[i, j]:
                    sel[i, j, off[i, j]] = 1.0
        for pid, p in enumerate(pats):
            if np.array_equal(p, sel):
                pat_ids.append(pid)
                break
        else:
            pat_ids.append(len(pats))
            pats.append(sel)
    c = np.arange(GRID_W)
    cs = np.clip(c - NA_COLS // 2, 0, GRID_W - NA_COLS)
    col_sel = np.zeros((GRID_W, GRID_W, n_coff), np.float32)
    for qc in range(GRID_W):
        for kc in range(cs[qc], cs[qc] + NA_COLS):
            col_sel[qc, kc, kc - qc + (NA_COLS - 1)] = 1.0
    return pat_ids, np.stack(pats), col_sel


def _na_dense_bias(rel_bias, row_sel, col_sel):
    npat = row_sel.shape[0]
    hp = lax.Precision.HIGHEST
    cols = jnp.einsum("hab,qkb->haqk", rel_bias.astype(F32), col_sel, precision=hp)
    dense = jnp.einsum("pija,haqk->phiqjk", row_sel, cols, precision=hp)
    valid = (row_sel.sum(-1) > 0)[:, None, :, None, :, None] & (col_sel.sum(-1) > 0)[None, None, None, :, None, :]
    dense = jnp.where(valid, dense, MASK_VALUE)
    return dense.reshape(npat, NA_HEADS, NA_TQ, NA_TK)


def _na_kernel(q_ref, k0, k1, k2, k3, v0, v1, v2, v3, z_ref, b_ref, o_ref):
    k_refs = (k0, k1, k2, k3)
    v_refs = (v0, v1, v2, v3)
    nt = (((1,), (1,)), ((), ()))
    for h in range(NA_HEADS):
        hs = slice(h * NA_HEAD_DIM, (h + 1) * NA_HEAD_DIM)
        q = q_ref[:, hs]
        s = jnp.concatenate(
            [lax.dot_general(q, kr[:, hs], nt, preferred_element_type=F32) for kr in k_refs],
            axis=-1)
        s = s + b_ref[h]
        m = jnp.max(s, axis=-1, keepdims=True)
        p = jnp.exp(s - m)
        l = jnp.sum(p, axis=-1, keepdims=True)
        pb = p.astype(BF16)
        o = jnp.zeros((NA_TQ, NA_HEAD_DIM), F32)
        for j, vr in enumerate(v_refs):
            o = o + jnp.dot(pb[:, j * NA_KBLK:(j + 1) * NA_KBLK], vr[:, hs],
                            preferred_element_type=F32)
        o = o / l
        o_ref[:, hs] = (o * _silu(z_ref[:, hs].astype(F32))).astype(BF16)


def _na_attention(proj, dense_bias, pat_ids, batch, seq):
    rows = seq // GRID_W
    nrb = rows // NA_QROWS
    t = batch * seq
    qblk_per_seq = seq // NA_TQ
    kblk_per_seq = seq // NA_KBLK
    kblk_per_row = NA_KBLK // GRID_W

    def q_map(col):
        return lambda rb, b: (b * qblk_per_seq + rb, col)

    def kv_map(col, j):
        def f(rb, b):
            start = _na_window_start(rb, rows) // kblk_per_row
            return (b * kblk_per_seq + start + j, col)
        return f

    in_specs = [pl.BlockSpec((NA_TQ, BRANCH_W), q_map(COL_NA_Q // BRANCH_W))]
    in_specs += [pl.BlockSpec((NA_KBLK, BRANCH_W), kv_map(COL_NA_K // BRANCH_W, j)) for j in range(NA_KSPLIT)]
    in_specs += [pl.BlockSpec((NA_KBLK, BRANCH_W), kv_map(COL_NA_V // BRANCH_W, j)) for j in range(NA_KSPLIT)]
    in_specs += [pl.BlockSpec((NA_TQ, BRANCH_W), q_map(COL_Z // BRANCH_W + 0))]
    def bias_map(rb, b):
        pid = 0
        for r, p in enumerate(pat_ids):
            pid = pid + jnp.where(rb == r, p, 0)
        return (pid, 0, 0, 0)

    in_specs += [pl.BlockSpec((None, NA_HEADS, NA_TQ, NA_TK), bias_map)]
    return pl.pallas_call(
        _na_kernel,
        grid=(nrb, batch),
        in_specs=in_specs,
        out_specs=pl.BlockSpec((NA_TQ, BRANCH_W), lambda rb, b: (b * qblk_per_seq + rb, 0)),
        out_shape=jax.ShapeDtypeStruct((t, BRANCH_W), BF16),
        compiler_params=_params(("parallel", "parallel")),
        name="na_attention",
    )(*([proj] * 10), dense_bias)


def _diff_kernel(lam_ref, sw_ref, q_ref, k_ref, v_ref, z_ref, o_ref, m_ref, l_ref, acc_ref,
                 *, tq, tk, lambda_init):
    h = pl.program_id(1)
    qi = pl.program_id(2)
    kj = pl.program_id(3)
    nt = (((1,), (1,)), ((), ()))

    @pl.when(kj == 0)
    def _():
        m_ref[...] = jnp.full(m_ref.shape, MASK_VALUE, F32)
        l_ref[...] = jnp.zeros(l_ref.shape, F32)
        acc_ref[...] = jnp.zeros(acc_ref.shape, F32)

    slope = jnp.where(h == 0, 0.25, jnp.where(h == 1, 0.0625, jnp.where(h == 2, 0.015625, 0.00390625)))
    qpos = qi * tq + lax.broadcasted_iota(jnp.int32, (tq, tk), 0)
    kpos = kj * tk + lax.broadcasted_iota(jnp.int32, (tq, tk), 1)
    bias = jnp.abs(qpos - kpos).astype(F32) * (-slope)

    q = q_ref[...]
    k = k_ref[...]
    v = v_ref[...]
    lane = lax.broadcasted_iota(jnp.int32, q.shape, 1)
    for c in range(2):
        in_c = (lane >= c * DIFF_HEAD_DIM) & (lane < (c + 1) * DIFF_HEAD_DIM)
        qc = jnp.where(in_c, q, jnp.zeros_like(q))
        s = lax.dot_general(qc, k, nt, preferred_element_type=F32) + bias
        m_prev = m_ref[c]
        m_new = jnp.maximum(m_prev, jnp.max(s, axis=-1, keepdims=True))
        alpha = jnp.exp(m_prev - m_new)
        p = jnp.exp(s - m_new[:, :1])
        l_ref[c] = alpha * l_ref[c] + jnp.sum(p, axis=-1, keepdims=True)
        acc_ref[c] = alpha * acc_ref[c] + jnp.dot(p.astype(BF16), v, preferred_element_type=F32)
        m_ref[c] = m_new

    @pl.when(kj == pl.num_programs(3) - 1)
    def _():
        lv = lam_ref[...]
        lam = (jnp.exp(jnp.sum(lv[0:1] * lv[1:2], axis=-1, keepdims=True))
               - jnp.exp(jnp.sum(lv[2:3] * lv[3:4], axis=-1, keepdims=True)) + lambda_init)
        a = acc_ref[0] / l_ref[0] - lam * (acc_ref[1] / l_ref[1])
        ms = jnp.mean(a * a, axis=-1, keepdims=True)
        y = a * lax.rsqrt(ms + EPS) * sw_ref[...] * (1.0 - lambda_init)
        o_ref[...] = (y * _silu(z_ref[...].astype(F32))).astype(BF16)


def _diff_attention(proj, lam_vecs, subln_w, batch, seq, lambda_init, tq=512, tk=1024):
    t = batch * seq
    nq = seq // tq
    nk = seq // tk
    hw = 2 * DIFF_HEAD_DIM
    kern = functools.partial(_diff_kernel, tq=tq, tk=tk, lambda_init=lambda_init)
    return pl.pallas_call(
        kern,
        grid=(batch, DIFF_HEADS, nq, nk),
        in_specs=[
            pl.BlockSpec((4, DIFF_HEAD_DIM), lambda b, h, i, j: (0, 0)),
            pl.BlockSpec((1, hw), lambda b, h, i, j: (0, 0)),
            pl.BlockSpec((tq, hw), lambda b, h, i, j: (b * nq + i, COL_DF_Q // hw + h)),
            pl.BlockSpec((tk, hw), lambda b, h, i, j: (b * nk + j, COL_DF_K // hw + h)),
            pl.BlockSpec((tk, hw), lambda b, h, i, j: (b * nk + j, COL_DF_V // hw + h)),
            pl.BlockSpec((tq, hw), lambda b, h, i, j: (b * nq + i, (COL_Z + BRANCH_W) // hw + h)),
        ],
        out_specs=pl.BlockSpec((tq, hw), lambda b, h, i, j: (b * nq + i, h)),
        out_shape=jax.ShapeDtypeStruct((t, BRANCH_W), BF16),
        scratch_shapes=[
            pltpu.VMEM((2, tq, hw), F32),
            pltpu.VMEM((2, tq, hw), F32),
            pltpu.VMEM((2, tq, hw), F32),
        ],
        compiler_params=_params(("parallel", "parallel", "parallel", "arbitrary")),
        name="diff_attention",
    )(lam_vecs, subln_w.reshape(1, hw), proj, proj, proj, proj)


def _pool_kernel(x_ref, z_ref, w_ref, sc_ref, o_ref, *, seq):
    t = lax.broadcasted_iota(jnp.int32, (seq, POOL_GROUP_W), 0)
    for g, win in enumerate(POOL_WINDOWS):
        gs = slice(g * POOL_GROUP_W, (g + 1) * POOL_GROUP_W)
        x = x_ref[:, gs].astype(F32)
        half = win // 2
        total = x
        for d in range(-half, win - half):
            if d == 0:
                continue
            shifted = pltpu.roll(x, (-d) % seq, axis=0)
            ok = (t + d >= 0) & (t + d < seq)
            total = total + jnp.where(ok, shifted, 0.0)
        cnt = jnp.minimum(t + (win - half), seq) - jnp.maximum(t - half, 0)
        pooled = total / cnt.astype(F32) - x
        y = jnp.dot(pooled.astype(BF16), w_ref[g], preferred_element_type=F32) * sc_ref[:, gs]
        o_ref[:, gs] = (y * _silu(z_ref[:, gs].astype(F32))).astype(BF16)


def _pool_mixer(proj, pool_w_bf, pool_scale, batch, seq):
    t = batch * seq
    return pl.pallas_call(
        functools.partial(_pool_kernel, seq=seq),
        grid=(batch,),
        in_specs=[
            pl.BlockSpec((seq, BRANCH_W), lambda b: (b, COL_POOL // BRANCH_W)),
            pl.BlockSpec((seq, BRANCH_W), lambda b: (b, COL_Z // BRANCH_W + 2)),
            pl.BlockSpec((len(POOL_WINDOWS), POOL_GROUP_W, POOL_GROUP_W), lambda b: (0, 0, 0)),
            pl.BlockSpec((1, BRANCH_W), lambda b: (0, 0)),
        ],
        out_specs=pl.BlockSpec((seq, BRANCH_W), lambda b: (b, 0)),
        out_shape=jax.ShapeDtypeStruct((t, BRANCH_W), BF16),
        compiler_params=_params(("parallel",)),
        name="pool_mixer",
    )(proj, proj, pool_w_bf, pool_scale.reshape(1, BRANCH_W))


def _sgu_kernel(u_ref, v_ref, z_ref, lw_ref, lb_ref, ws_ref, bs_ref, o_ref, *, tm):
    v = v_ref[...].astype(F32)
    mu = jnp.mean(v, axis=-1, keepdims=True)
    vc = v - mu
    var = jnp.mean(vc * vc, axis=-1, keepdims=True)
    vn = (vc * lax.rsqrt(var + EPS) * lw_ref[...] + lb_ref[...]).astype(BF16)
    for n in range(tm // SGU_CHUNK):
        rs = slice(n * SGU_CHUNK, (n + 1) * SGU_CHUNK)
        for g in range(SGU_GROUPS):
            gs = slice(g * SGU_GROUP_W, (g + 1) * SGU_GROUP_W)
            mixed = jnp.dot(ws_ref[g], vn[rs, gs], preferred_element_type=F32) + bs_ref[g]
            y = u_ref[rs, gs].astype(F32) * mixed
            o_ref[rs, gs] = (y * _silu(z_ref[rs, gs].astype(F32))).astype(BF16)


def _sgu_mixer(proj, ln_w, ln_b, ws_bf, bs_full, t, tm=512):
    return pl.pallas_call(
        functools.partial(_sgu_kernel, tm=tm),
        grid=(t // tm,),
        in_specs=[
            pl.BlockSpec((tm, BRANCH_W), lambda i: (i, COL_SGU_U // BRANCH_W)),
            pl.BlockSpec((tm, BRANCH_W), lambda i: (i, COL_SGU_V // BRANCH_W)),
            pl.BlockSpec((tm, BRANCH_W), lambda i: (i, COL_Z // BRANCH_W + 3)),
            pl.BlockSpec((1, BRANCH_W), lambda i: (0, 0)),
            pl.BlockSpec((1, BRANCH_W), lambda i: (0, 0)),
            pl.BlockSpec((SGU_GROUPS, SGU_CHUNK, SGU_CHUNK), lambda i: (0, 0, 0)),
            pl.BlockSpec((SGU_GROUPS, SGU_CHUNK, SGU_GROUP_W), lambda i: (0, 0, 0)),
        ],
        out_specs=pl.BlockSpec((tm, BRANCH_W), lambda i: (i, 0)),
        out_shape=jax.ShapeDtypeStruct((t, BRANCH_W), BF16),
        compiler_params=_params(("parallel",)),
        name="sgu_mixer",
    )(proj, proj, proj, ln_w.reshape(1, BRANCH_W), ln_b.reshape(1, BRANCH_W), ws_bf, bs_full)


def _merge_kernel(h_ref, ya_ref, yb_ref, yc_ref, yd_ref, wg_ref, wb_ref, o_ref):
    h = h_ref[...]
    merged = None
    for i, y_ref in enumerate((ya_ref, yb_ref, yc_ref, yd_ref)):
        gate = _sigmoid(jnp.dot(h, wg_ref[i], preferred_element_type=F32))
        term = gate * jnp.dot(y_ref[...], wb_ref[i], preferred_element_type=F32)
        merged = term if merged is None else merged + term
    o_ref[...] = merged.astype(BF16)


def _merge(h, ys, wg_bf, wb_bf, tm=1024, tn=512):
    t = h.shape[0]
    y_spec = pl.BlockSpec((tm, BRANCH_W), lambda i, j: (i, 0))
    return pl.pallas_call(
        _merge_kernel,
        grid=(t // tm, D_MODEL // tn),
        in_specs=[
            pl.BlockSpec((tm, D_MODEL), lambda i, j: (i, 0)),
            y_spec, y_spec, y_spec, y_spec,
            pl.BlockSpec((N_BRANCH, D_MODEL, tn), lambda i, j: (0, 0, j)),
            pl.BlockSpec((N_BRANCH, BRANCH_W, tn), lambda i, j: (0, 0, j)),
        ],
        out_specs=pl.BlockSpec((tm, tn), lambda i, j: (i, j)),
        out_shape=jax.ShapeDtypeStruct((t, D_MODEL), BF16),
        compiler_params=_params(("parallel", "arbitrary")),
        name="merge",
    )(h, *ys, wg_bf, wb_bf)


def _outproj_kernel(x_ref, m_ref, w_ref, fw_ref, o_ref, *, final_norm):
    x = x_ref[...] + jnp.dot(m_ref[...], w_ref[...], preferred_element_type=F32)
    if final_norm:
        ms = jnp.mean(x * x, axis=-1, keepdims=True)
        x = x * lax.rsqrt(ms + EPS) * fw_ref[...]
    o_ref[...] = x


def _outproj(x2, merged, w_out_bf, final_w, final_norm, tm=512):
    t = x2.shape[0]
    return pl.pallas_call(
        functools.partial(_outproj_kernel, final_norm=final_norm),
        grid=(t // tm,),
        in_specs=[
            pl.BlockSpec((tm, D_MODEL), lambda i: (i, 0)),
            pl.BlockSpec((tm, D_MODEL), lambda i: (i, 0)),
            pl.BlockSpec((D_MODEL, D_MODEL), lambda i: (0, 0)),
            pl.BlockSpec((1, D_MODEL), lambda i: (0, 0)),
        ],
        out_specs=pl.BlockSpec((tm, D_MODEL), lambda i: (i, 0)),
        out_shape=jax.ShapeDtypeStruct((t, D_MODEL), F32),
        compiler_params=_params(("parallel",)),
        name="outproj",
    )(x2, merged, w_out_bf, final_w.reshape(1, D_MODEL))


def _column_scale():
    cs = jnp.ones((IN_COLS,), F32)
    cs = cs.at[COL_NA_Q:COL_NA_Q + BRANCH_W].set(NA_HEAD_DIM ** -0.5)
    cs = cs.at[COL_DF_Q:COL_DF_Q + BRANCH_W].set(DIFF_HEAD_DIM ** -0.5)
    return cs.reshape(1, IN_COLS)


def kernel(x, norm_w, w_in, na_bias, lam_q1, lam_k1, lam_q2, lam_k2, diff_subln_w, pool_w, pool_scale,
           sgu_ln_w, sgu_ln_b, sgu_ws, sgu_b, w_branch, w_gate, w_out, final_norm_w):
    batch, seq, _ = x.shape
    depth = norm_w.shape[0]
    t = batch * seq
    rows = seq // GRID_W
    x2 = x.reshape(t, D_MODEL)
    col_scale = _column_scale()
    pat_ids, row_sel, col_sel = _na_patterns(rows)
    for l in range(depth):
        h, proj = _inproj(x2, norm_w[l], w_in[l].astype(BF16), col_scale)
        y_a = _na_attention(proj, _na_dense_bias(na_bias[l], row_sel, col_sel), pat_ids, batch, seq)
        lambda_init = 0.8 - 0.6 * math.exp(-0.3 * l)
        lam_vecs = jnp.stack([lam_q1[l], lam_k1[l], lam_q2[l], lam_k2[l]]).astype(F32)
        y_b = _diff_attention(proj, lam_vecs, diff_subln_w[l], batch, seq, lambda_init)
        y_c = _pool_mixer(proj, pool_w[l].astype(BF16), pool_scale[l], batch, seq)
        bs_full = jnp.broadcast_to(sgu_b[l][:, :, None], (SGU_GROUPS, SGU_CHUNK, SGU_GROUP_W))
        y_d = _sgu_mixer(proj, sgu_ln_w[l], sgu_ln_b[l], sgu_ws[l].astype(BF16), bs_full, t)
        merged = _merge(h, (y_a, y_b, y_c, y_d), w_gate[l].astype(BF16), w_branch[l].astype(BF16))
        x2 = _outproj(x2, merged, w_out[l].astype(BF16), final_norm_w, l == depth - 1)
    return x2.reshape(batch, seq, D_MODEL)
```

```python
import functools
import math

import jax
import jax.numpy as jnp
import numpy as np
from jax import lax
from jax.experimental import pallas as pl
from jax.experimental.pallas import tpu as pltpu

D_MODEL = 2048
GRID_W = 64
N_BRANCH = 4
BRANCH_W = D_MODEL // N_BRANCH
NA_HEADS = 4
NA_HEAD_DIM = BRANCH_W // NA_HEADS
NA_ROWS_MAX = 8
NA_COLS = 16
DIFF_HEADS = 4
DIFF_HEAD_DIM = BRANCH_W // (2 * DIFF_HEADS)
POOL_WINDOWS = (2, 4, 8, 16)
POOL_GROUP_W = BRANCH_W // len(POOL_WINDOWS)
SGU_CHUNK = 128
SGU_GROUPS = 4
SGU_GROUP_W = BRANCH_W // SGU_GROUPS
IN_COLS = 13 * BRANCH_W
EPS = 1e-6

COL_NA_Q = 0
COL_NA_K = BRANCH_W
COL_NA_V = 2 * BRANCH_W
COL_DF_Q = 3 * BRANCH_W
COL_DF_K = 4 * BRANCH_W
COL_DF_V = 5 * BRANCH_W
COL_POOL = 6 * BRANCH_W
COL_SGU_U = 7 * BRANCH_W
COL_SGU_V = 8 * BRANCH_W
COL_Z = 9 * BRANCH_W

LANES = 128
VMEM_LIMIT = 56 * 1024 * 1024
MASK_VALUE = -1e30

BF16 = jnp.bfloat16
F32 = jnp.float32


def _params(sem, vmem=VMEM_LIMIT):
    return pltpu.CompilerParams(dimension_semantics=sem, vmem_limit_bytes=vmem)


def _silu(z):
    return z * (1.0 / (1.0 + jnp.exp(-z)))


def _sigmoid(z):
    return 1.0 / (1.0 + jnp.exp(-z))


def _inproj_kernel(x_ref, nw_ref, w_ref, cs_ref, h_ref, p_ref):
    @pl.when(pl.program_id(1) == 0)
    def _():
        x = x_ref[...]
        ms = jnp.mean(x * x, axis=-1, keepdims=True)
        h_ref[...] = (x * lax.rsqrt(ms + EPS) * nw_ref[...]).astype(BF16)

    acc = jnp.dot(h_ref[...], w_ref[...], preferred_element_type=F32)
    p_ref[...] = (acc * cs_ref[...]).astype(BF16)


def _inproj(x2, norm_w, w_in_bf, col_scale, tm=1024, tn=IN_COLS // 4):
    t = x2.shape[0]
    return pl.pallas_call(
        _inproj_kernel,
        grid=(t // tm, IN_COLS // tn),
        in_specs=[
            pl.BlockSpec((tm, D_MODEL), lambda i, j: (i, 0)),
            pl.BlockSpec((1, D_MODEL), lambda i, j: (0, 0)),
            pl.BlockSpec((D_MODEL, tn), lambda i, j: (0, j)),
            pl.BlockSpec((1, tn), lambda i, j: (0, j)),
        ],
        out_specs=[
            pl.BlockSpec((tm, D_MODEL), lambda i, j: (i, 0)),
            pl.BlockSpec((tm, tn), lambda i, j: (i, j)),
        ],
        out_shape=[
            jax.ShapeDtypeStruct((t, D_MODEL), BF16),
            jax.ShapeDtypeStruct((t, IN_COLS), BF16),
        ],
        compiler_params=_params(("parallel", "arbitrary")),
        name="inproj",
    )(x2, norm_w.reshape(1, D_MODEL), w_in_bf, col_scale)


NA_QROWS = 8
NA_KROWS = 16
NA_TQ = NA_QROWS * GRID_W
NA_TK = NA_KROWS * GRID_W
NA_KSPLIT = 4
NA_KBLK = NA_TK // NA_KSPLIT


def _na_window_start(rb, rows):
    lo = rb * NA_QROWS - NA_ROWS_MAX // 2
    return jnp.clip(lo, 0, rows - NA_KROWS)


def _na_patterns(rows):
    nrb = rows // NA_QROWS
    kr_win = min(NA_ROWS_MAX, rows)
    n_roff = 2 * NA_ROWS_MAX - 1
    n_coff = 2 * NA_COLS - 1
    pats, pat_ids = [], []
    for rb in range(nrb):
        r = rb * NA_QROWS + np.arange(NA_QROWS)
        start = int(np.clip(rb * NA_QROWS - NA_ROWS_MAX // 2, 0, rows - NA_KROWS))
        kr = start + np.arange(NA_KROWS)
        rs = np.clip(r - kr_win // 2, 0, rows - kr_win)
        ok = (kr[None, :] >= rs[:, None]) & (kr[None, :] < rs[:, None] + kr_win)
        off = kr[None, :] - r[:, None] + (NA_ROWS_MAX - 1)
        sel = np.zeros((NA_QROWS, NA_KROWS, n_roff), np.float32)
        for i in range(NA_QROWS):
            for j in range(NA_KROWS):
                if ok[i, j]:
                    sel[i, j, off[i, j]] = 1.0
        for pid, p in enumerate(pats):
            if np.array_equal(p, sel):
                pat_ids.append(pid)
                break
        else:
            pat_ids.append(len(pats))
            pats.append(sel)
    c = np.arange(GRID_W)
    cs = np.clip(c - NA_COLS // 2, 0, GRID_W - NA_COLS)
    col_sel = np.zeros((GRID_W, GRID_W, n_coff), np.float32)
    for qc in range(GRID_W):
        for kc in range(cs[qc], cs[qc] + NA_COLS):
            col_sel[qc, kc, kc - qc + (NA_COLS - 1)] = 1.0
    return pat_ids, np.stack(pats), col_sel


def _na_dense_bias(rel_bias, row_sel, col_sel):
    npat = row_sel.shape[0]
    hp = lax.Precision.HIGHEST
    cols = jnp.einsum("hab,qkb->haqk", rel_bias.astype(F32), col_sel, precision=hp)
    dense = jnp.einsum("pija,haqk->phiqjk", row_sel, cols, precision=hp)
    valid = (row_sel.sum(-1) > 0)[:, None, :, None, :, None] & (col_sel.sum(-1) > 0)[None, None, None, :, None, :]
    dense = jnp.where(valid, dense, MASK_VALUE)
    return dense.reshape(npat, NA_HEADS, NA_TQ, NA_TK)


def _na_kernel(q_ref, k0, k1, k2, k3, v0, v1, v2, v3, z_ref, b_ref, o_ref):
    k_refs = (k0, k1, k2, k3)
    v_refs = (v0, v1, v2, v3)
    nt = (((1,), (1,)), ((), ()))
    for h in range(NA_HEADS):
        hs = slice(h * NA_HEAD_DIM, (h + 1) * NA_HEAD_DIM)
        q = q_ref[:, hs]
        s = jnp.concatenate(
            [lax.dot_general(q, kr[:, hs], nt, preferred_element_type=F32) for kr in k_refs],
            axis=-1)
        s = s + b_ref[h]
        m = jnp.max(s, axis=-1, keepdims=True)
        p = jnp.exp(s - m)
        l = jnp.sum(p, axis=-1, keepdims=True)
        pb = p.astype(BF16)
        o = jnp.zeros((NA_TQ, NA_HEAD_DIM), F32)
        for j, vr in enumerate(v_refs):
            o = o + jnp.dot(pb[:, j * NA_KBLK:(j + 1) * NA_KBLK], vr[:, hs],
                            preferred_element_type=F32)
        o = o / l
        o_ref[:, hs] = (o * _silu(z_ref[:, hs].astype(F32))).astype(BF16)


def _na_attention(proj, dense_bias, pat_ids, batch, seq):
    rows = seq // GRID_W
    nrb = rows // NA_QROWS
    t = batch * seq
    qblk_per_seq = seq // NA_TQ
    kblk_per_seq = seq // NA_KBLK
    kblk_per_row = NA_KBLK // GRID_W

    def q_map(col):
        return lambda rb, b: (b * qblk_per_seq + rb, col)

    def kv_map(col, j):
        def f(rb, b):
            start = _na_window_start(rb, rows) // kblk_per_row
            return (b * kblk_per_seq + start + j, col)
        return f

    in_specs = [pl.BlockSpec((NA_TQ, BRANCH_W), q_map(COL_NA_Q // BRANCH_W))]
    in_specs += [pl.BlockSpec((NA_KBLK, BRANCH_W), kv_map(COL_NA_K // BRANCH_W, j)) for j in range(NA_KSPLIT)]
    in_specs += [pl.BlockSpec((NA_KBLK, BRANCH_W), kv_map(COL_NA_V // BRANCH_W, j)) for j in range(NA_KSPLIT)]
    in_specs += [pl.BlockSpec((NA_TQ, BRANCH_W), q_map(COL_Z // BRANCH_W + 0))]
    def bias_map(rb, b):
        pid = 0
        for r, p in enumerate(pat_ids):
            pid = pid + jnp.where(rb == r, p, 0)
        return (pid, 0, 0, 0)

    in_specs += [pl.BlockSpec((None, NA_HEADS, NA_TQ, NA_TK), bias_map)]
    return pl.pallas_call(
        _na_kernel,
        grid=(nrb, batch),
        in_specs=in_specs,
        out_specs=pl.BlockSpec((NA_TQ, BRANCH_W), lambda rb, b: (b * qblk_per_seq + rb, 0)),
        out_shape=jax.ShapeDtypeStruct((t, BRANCH_W), BF16),
        compiler_params=_params(("parallel", "parallel")),
        name="na_attention",
    )(*([proj] * 10), dense_bias)


DIFF_QCHUNK = 512
DIFF_KCHUNK = 1024
LOG2E = math.log2(math.e)


def _alibi_slopes():
    return [2.0 ** (-8.0 * (h + 1) / DIFF_HEADS) * LOG2E for h in range(DIFF_HEADS)]


def _split3(x):
    def trunc(v):
        return (v.view(np.uint32) & np.uint32(0xFFFF0000)).view(np.float32)
    x1 = trunc(x)
    r1 = x - x1
    x2 = trunc(r1)
    return x1, x2, r1 - x2


def _alibi_features(seq):
    pos = np.arange(seq, dtype=np.float32)
    slopes = np.asarray(_alibi_slopes(), np.float32)[:, None]
    u = _split3(-slopes * pos[None, :])
    w = _split3(slopes * pos[None, :])
    one = np.ones((DIFF_HEADS, seq), np.float32)
    zero = np.zeros((DIFF_HEADS, seq, LANES - 6), np.float32)
    qfeat = np.concatenate([np.stack(list(u) + [one] * 3, axis=-1), zero], axis=-1)
    kfeat = np.concatenate([np.stack([one] * 3 + list(w), axis=-1), zero], axis=-1)
    return jnp.asarray(qfeat).astype(BF16), jnp.asarray(kfeat).astype(BF16)


def _diff_kernel(lam_ref, sw_ref, q_ref, qf_ref, k_ref, kf_ref, vt_ref, corr_ref, z_ref, o_ref,
                 qa_ref, m_ref, l_ref, acc_ref, *, tq, tk, lambda_init):
    h = pl.program_id(1)
    qi = pl.program_id(2)
    kj = pl.program_id(3)
    nt = (((1,), (1,)), ((), ()))
    hw = 2 * DIFF_HEAD_DIM

    @pl.when(kj == 0)
    def _():
        m_ref[...] = jnp.full(m_ref.shape, MASK_VALUE, F32)
        l_ref[...] = jnp.zeros(l_ref.shape, F32)
        acc_ref[...] = jnp.zeros(acc_ref.shape, F32)
        q = q_ref[...]
        lane = lax.broadcasted_iota(jnp.int32, q.shape, 1)
        for c in range(2):
            in_c = (lane >= c * DIFF_HEAD_DIM) & (lane < (c + 1) * DIFF_HEAD_DIM)
            qa_ref[c, :, :hw] = jnp.where(in_c, q, jnp.zeros_like(q))
            qa_ref[c, :, hw:] = qf_ref[...]

    q0 = qi * tq
    k0 = kj * tk
    keys_after = k0 >= q0 + tq - 1
    keys_before = k0 + tk - 1 <= q0
    kf = kf_ref[...]
    ka = jnp.concatenate([k_ref[...], jnp.where(keys_after, -kf, kf)], axis=1)
    vt = vt_ref[...]

    def step(mixed):
        chunks = [(kk, c, n) for kk in range(tk // DIFF_KCHUNK) for c in range(2)
                  for n in range(tq // DIFF_QCHUNK)]

        def slices(kk, c, n):
            return (slice(kk * DIFF_KCHUNK, (kk + 1) * DIFF_KCHUNK),
                    slice(n * DIFF_QCHUNK, (n + 1) * DIFF_QCHUNK))

        def scores(kk, c, n):
            ks, cs = slices(kk, c, n)
            st = lax.dot_general(ka[ks], qa_ref[c, cs, :], nt, preferred_element_type=F32)
            if mixed:
                st = st + corr_ref[ks, cs]
            return st

        def softmax(st, kk, c, n):
            _, cs = slices(kk, c, n)
            m_prev = m_ref[c, :, cs]
            m_new = jnp.maximum(m_prev, jnp.max(st, axis=0, keepdims=True))
            alpha = jnp.exp2(m_prev - m_new)
            p = jnp.exp2(st - m_new)
            l_ref[c, :, cs] = alpha * l_ref[c, :, cs] + jnp.sum(p, axis=0, keepdims=True)
            m_ref[c, :, cs] = m_new
            return alpha, p.astype(BF16)

        def accumulate(alpha, p, kk, c, n):
            ks, cs = slices(kk, c, n)
            acc_ref[c, :, cs] = alpha * acc_ref[c, :, cs] + jnp.dot(vt[:, ks], p, preferred_element_type=F32)

        nc = len(chunks)
        st_q = {0: scores(*chunks[0])}
        if nc > 1:
            st_q[1] = scores(*chunks[1])
        p_q = {0: softmax(st_q.pop(0), *chunks[0])}
        for i in range(nc):
            if i + 2 < nc:
                st_q[i + 2] = scores(*chunks[i + 2])
            if i + 1 < nc:
                p_q[i + 1] = softmax(st_q.pop(i + 1), *chunks[i + 1])
            accumulate(*p_q.pop(i), *chunks[i])

    pure = keys_after | keys_before
    pl.when(pure)(lambda: step(False))
    pl.when(jnp.logical_not(pure))(lambda: step(True))

    @pl.when(kj == pl.num_programs(3) - 1)
    def _():
        lv = lam_ref[...]
        lam = (jnp.exp(jnp.sum(lv[0:1] * lv[1:2], axis=-1, keepdims=True))
               - jnp.exp(jnp.sum(lv[2:3] * lv[3:4], axis=-1, keepdims=True)) + lambda_init)
        at = acc_ref[0] / l_ref[0] - lam * (acc_ref[1] / l_ref[1])
        ms = jnp.mean(at * at, axis=0, keepdims=True)
        y = (at * lax.rsqrt(ms + EPS)).T * sw_ref[...] * (1.0 - lambda_init)
        o_ref[...] = (y * _silu(z_ref[...].astype(F32))).astype(BF16)


def _diff_attention(proj, lam_vecs, subln_w, batch, seq, lambda_init, tq=1024, tk=1024):
    t = batch * seq
    nq = seq // tq
    nk = seq // tk
    hw = 2 * DIFF_HEAD_DIM
    assert tq == tk, "the diagonal correction tile assumes aligned square tiles"
    qfeat, kfeat = _alibi_features(seq)
    rel = jnp.arange(tk, dtype=F32)[:, None] - jnp.arange(tq, dtype=F32)[None, :]
    corr = -2.0 * jnp.asarray(_alibi_slopes(), F32)[:, None, None] * jnp.maximum(rel, 0.0)[None]
    v_t = proj[:, COL_DF_V:COL_DF_V + BRANCH_W].T
    kern = functools.partial(_diff_kernel, tq=tq, tk=tk, lambda_init=lambda_init)
    return pl.pallas_call(
        kern,
        grid=(batch, DIFF_HEADS, nq, nk),
        in_specs=[
            pl.BlockSpec((4, DIFF_HEAD_DIM), lambda b, h, i, j: (0, 0)),
            pl.BlockSpec((1, hw), lambda b, h, i, j: (0, 0)),
            pl.BlockSpec((tq, hw), lambda b, h, i, j: (b * nq + i, COL_DF_Q // hw + h)),
            pl.BlockSpec((None, tq, LANES), lambda b, h, i, j: (h, i, 0)),
            pl.BlockSpec((tk, hw), lambda b, h, i, j: (b * nk + j, COL_DF_K // hw + h)),
            pl.BlockSpec((None, tk, LANES), lambda b, h, i, j: (h, j, 0)),
            pl.BlockSpec((hw, tk), lambda b, h, i, j: (h, b * nk + j)),
            pl.BlockSpec((None, tk, tq), lambda b, h, i, j: (h, 0, 0)),
            pl.BlockSpec((tq, hw), lambda b, h, i, j: (b * nq + i, (COL_Z + BRANCH_W) // hw + h)),
        ],
        out_specs=pl.BlockSpec((tq, hw), lambda b, h, i, j: (b * nq + i, h)),
        out_shape=jax.ShapeDtypeStruct((t, BRANCH_W), BF16),
        scratch_shapes=[
            pltpu.VMEM((2, tq, 2 * hw), BF16),
            pltpu.VMEM((2, 1, tq), F32),
            pltpu.VMEM((2, 1, tq), F32),
            pltpu.VMEM((2, hw, tq), F32),
        ],
        compiler_params=_params(("parallel", "parallel", "parallel", "arbitrary")),
        name="diff_attention",
    )(lam_vecs, subln_w.reshape(1, hw), proj, qfeat, proj, kfeat, v_t, corr, proj)


def _pool_kernel(x_ref, z_ref, w_ref, sc_ref, o_ref, *, seq):
    t = lax.broadcasted_iota(jnp.int32, (seq, POOL_GROUP_W), 0)
    for g, win in enumerate(POOL_WINDOWS):
        gs = slice(g * POOL_GROUP_W, (g + 1) * POOL_GROUP_W)
        x = x_ref[:, gs].astype(F32)
        half = win // 2
        total = x
        for d in range(-half, win - half):
            if d == 0:
                continue
            shifted = pltpu.roll(x, (-d) % seq, axis=0)
            ok = (t + d >= 0) & (t + d < seq)
            total = total + jnp.where(ok, shifted, 0.0)
        cnt = jnp.minimum(t + (win - half), seq) - jnp.maximum(t - half, 0)
        pooled = total / cnt.astype(F32) - x
        y = jnp.dot(pooled.astype(BF16), w_ref[g], preferred_element_type=F32) * sc_ref[:, gs]
        o_ref[:, gs] = (y * _silu(z_ref[:, gs].astype(F32))).astype(BF16)


def _pool_mixer(proj, pool_w_bf, pool_scale, batch, seq):
    t = batch * seq
    return pl.pallas_call(
        functools.partial(_pool_kernel, seq=seq),
        grid=(batch,),
        in_specs=[
            pl.BlockSpec((seq, BRANCH_W), lambda b: (b, COL_POOL // BRANCH_W)),
            pl.BlockSpec((seq, BRANCH_W), lambda b: (b, COL_Z // BRANCH_W + 2)),
            pl.BlockSpec((len(POOL_WINDOWS), POOL_GROUP_W, POOL_GROUP_W), lambda b: (0, 0, 0)),
            pl.BlockSpec((1, BRANCH_W), lambda b: (0, 0)),
        ],
        out_specs=pl.BlockSpec((seq, BRANCH_W), lambda b: (b, 0)),
        out_shape=jax.ShapeDtypeStruct((t, BRANCH_W), BF16),
        compiler_params=_params(("parallel",)),
        name="pool_mixer",
    )(proj, proj, pool_w_bf, pool_scale.reshape(1, BRANCH_W))


def _sgu_kernel(u_ref, v_ref, z_ref, lw_ref, lb_ref, ws_ref, bs_ref, o_ref, *, tm):
    v = v_ref[...].astype(F32)
    mu = jnp.mean(v, axis=-1, keepdims=True)
    vc = v - mu
    var = jnp.mean(vc * vc, axis=-1, keepdims=True)
    vn = (vc * lax.rsqrt(var + EPS) * lw_ref[...] + lb_ref[...]).astype(BF16)
    for n in range(tm // SGU_CHUNK):
        rs = slice(n * SGU_CHUNK, (n + 1) * SGU_CHUNK)
        for g in range(SGU_GROUPS):
            gs = slice(g * SGU_GROUP_W, (g + 1) * SGU_GROUP_W)
            mixed = jnp.dot(ws_ref[g], vn[rs, gs], preferred_element_type=F32) + bs_ref[g]
            y = u_ref[rs, gs].astype(F32) * mixed
            o_ref[rs, gs] = (y * _silu(z_ref[rs, gs].astype(F32))).astype(BF16)


def _sgu_mixer(proj, ln_w, ln_b, ws_bf, bs_full, t, tm=512):
    return pl.pallas_call(
        functools.partial(_sgu_kernel, tm=tm),
        grid=(t // tm,),
        in_specs=[
            pl.BlockSpec((tm, BRANCH_W), lambda i: (i, COL_SGU_U // BRANCH_W)),
            pl.BlockSpec((tm, BRANCH_W), lambda i: (i, COL_SGU_V // BRANCH_W)),
            pl.BlockSpec((tm, BRANCH_W), lambda i: (i, COL_Z // BRANCH_W + 3)),
            pl.BlockSpec((1, BRANCH_W), lambda i: (0, 0)),
            pl.BlockSpec((1, BRANCH_W), lambda i: (0, 0)),
            pl.BlockSpec((SGU_GROUPS, SGU_CHUNK, SGU_CHUNK), lambda i: (0, 0, 0)),
            pl.BlockSpec((SGU_GROUPS, SGU_CHUNK, SGU_GROUP_W), lambda i: (0, 0, 0)),
        ],
        out_specs=pl.BlockSpec((tm, BRANCH_W), lambda i: (i, 0)),
        out_shape=jax.ShapeDtypeStruct((t, BRANCH_W), BF16),
        compiler_params=_params(("parallel",)),
        name="sgu_mixer",
    )(proj, proj, proj, ln_w.reshape(1, BRANCH_W), ln_b.reshape(1, BRANCH_W), ws_bf, bs_full)


def _merge_kernel(h_ref, ya_ref, yb_ref, yc_ref, yd_ref, wg_ref, wb_ref, o_ref):
    h = h_ref[...]
    merged = None
    for i, y_ref in enumerate((ya_ref, yb_ref, yc_ref, yd_ref)):
        gate = _sigmoid(jnp.dot(h, wg_ref[i], preferred_element_type=F32))
        term = gate * jnp.dot(y_ref[...], wb_ref[i], preferred_element_type=F32)
        merged = term if merged is None else merged + term
    o_ref[...] = merged.astype(BF16)


def _merge(h, ys, wg_bf, wb_bf, tm=1024, tn=512):
    t = h.shape[0]
    y_spec = pl.BlockSpec((tm, BRANCH_W), lambda i, j: (i, 0))
    return pl.pallas_call(
        _merge_kernel,
        grid=(t // tm, D_MODEL // tn),
        in_specs=[
            pl.BlockSpec((tm, D_MODEL), lambda i, j: (i, 0)),
            y_spec, y_spec, y_spec, y_spec,
            pl.BlockSpec((N_BRANCH, D_MODEL, tn), lambda i, j: (0, 0, j)),
            pl.BlockSpec((N_BRANCH, BRANCH_W, tn), lambda i, j: (0, 0, j)),
        ],
        out_specs=pl.BlockSpec((tm, tn), lambda i, j: (i, j)),
        out_shape=jax.ShapeDtypeStruct((t, D_MODEL), BF16),
        compiler_params=_params(("parallel", "arbitrary")),
        name="merge",
    )(h, *ys, wg_bf, wb_bf)


def _outproj_kernel(x_ref, m_ref, w_ref, fw_ref, o_ref, *, final_norm):
    x = x_ref[...] + jnp.dot(m_ref[...], w_ref[...], preferred_element_type=F32)
    if final_norm:
        ms = jnp.mean(x * x, axis=-1, keepdims=True)
        x = x * lax.rsqrt(ms + EPS) * fw_ref[...]
    o_ref[...] = x


def _outproj(x2, merged, w_out_bf, final_w, final_norm, tm=512):
    t = x2.shape[0]
    return pl.pallas_call(
        functools.partial(_outproj_kernel, final_norm=final_norm),
        grid=(t // tm,),
        in_specs=[
            pl.BlockSpec((tm, D_MODEL), lambda i: (i, 0)),
            pl.BlockSpec((tm, D_MODEL), lambda i: (i, 0)),
            pl.BlockSpec((D_MODEL, D_MODEL), lambda i: (0, 0)),
            pl.BlockSpec((1, D_MODEL), lambda i: (0, 0)),
        ],
        out_specs=pl.BlockSpec((tm, D_MODEL), lambda i: (i, 0)),
        out_shape=jax.ShapeDtypeStruct((t, D_MODEL), F32),
        compiler_params=_params(("parallel",)),
        name="outproj",
    )(x2, merged, w_out_bf, final_w.reshape(1, D_MODEL))


def _column_scale():
    cs = jnp.ones((IN_COLS,), F32)
    cs = cs.at[COL_NA_Q:COL_NA_Q + BRANCH_W].set(NA_HEAD_DIM ** -0.5)
    cs = cs.at[COL_DF_Q:COL_DF_Q + BRANCH_W].set(DIFF_HEAD_DIM ** -0.5 * LOG2E)
    return cs.reshape(1, IN_COLS)


def kernel(x, norm_w, w_in, na_bias, lam_q1, lam_k1, lam_q2, lam_k2, diff_subln_w, pool_w, pool_scale,
           sgu_ln_w, sgu_ln_b, sgu_ws, sgu_b, w_branch, w_gate, w_out, final_norm_w):
    batch, seq, _ = x.shape
    depth = norm_w.shape[0]
    t = batch * seq
    rows = seq // GRID_W
    x2 = x.reshape(t, D_MODEL)
    col_scale = _column_scale()
    pat_ids, row_sel, col_sel = _na_patterns(rows)
    for l in range(depth):
        h, proj = _inproj(x2, norm_w[l], w_in[l].astype(BF16), col_scale)
        y_a = _na_attention(proj, _na_dense_bias(na_bias[l], row_sel, col_sel), pat_ids, batch, seq)
        lambda_init = 0.8 - 0.6 * math.exp(-0.3 * l)
        lam_vecs = jnp.stack([lam_q1[l], lam_k1[l], lam_q2[l], lam_k2[l]]).astype(F32)
        y_b = _diff_attention(proj, lam_vecs, diff_subln_w[l], batch, seq, lambda_init)
        y_c = _pool_mixer(proj, pool_w[l].astype(BF16), pool_scale[l], batch, seq)
        bs_full = jnp.broadcast_to(sgu_b[l][:, :, None], (SGU_GROUPS, SGU_CHUNK, SGU_GROUP_W))
        y_d = _sgu_mixer(proj, sgu_ln_w[l], sgu_ln_b[l], sgu_ws[l].astype(BF16), bs_full, t)
        merged = _merge(h, (y_a, y_b, y_c, y_d), w_gate[l].astype(BF16), w_branch[l].astype(BF16))
        x2 = _outproj(x2, merged, w_out[l].astype(BF16), final_norm_w, l == depth - 1)
    return x2.reshape(batch, seq, D_MODEL)
```

```python
import functools
import math

import jax
import jax.numpy as jnp
import numpy as np
from jax import lax
from jax.experimental import pallas as pl
from jax.experimental.pallas import tpu as pltpu

D_MODEL = 2048
GRID_W = 64
N_BRANCH = 4
BRANCH_W = D_MODEL // N_BRANCH
NA_HEADS = 4
NA_HEAD_DIM = BRANCH_W // NA_HEADS
NA_ROWS_MAX = 8
NA_COLS = 16
DIFF_HEADS = 4
DIFF_HEAD_DIM = BRANCH_W // (2 * DIFF_HEADS)
POOL_WINDOWS = (2, 4, 8, 16)
POOL_GROUP_W = BRANCH_W // len(POOL_WINDOWS)
SGU_CHUNK = 128
SGU_GROUPS = 4
SGU_GROUP_W = BRANCH_W // SGU_GROUPS
IN_COLS = 13 * BRANCH_W
EPS = 1e-6

COL_NA_Q = 0
COL_NA_K = BRANCH_W
COL_NA_V = 2 * BRANCH_W
COL_DF_Q = 3 * BRANCH_W
COL_DF_K = 4 * BRANCH_W
COL_DF_V = 5 * BRANCH_W
COL_POOL = 6 * BRANCH_W
COL_SGU_U = 7 * BRANCH_W
COL_SGU_V = 8 * BRANCH_W
COL_Z = 9 * BRANCH_W

LANES = 128
VMEM_LIMIT = 56 * 1024 * 1024
MASK_VALUE = -1e30

BF16 = jnp.bfloat16
F32 = jnp.float32


def _params(sem, vmem=VMEM_LIMIT):
    return pltpu.CompilerParams(dimension_semantics=sem, vmem_limit_bytes=vmem)


def _silu(z):
    return z * (1.0 / (1.0 + jnp.exp(-z)))


def _sigmoid(z):
    return 1.0 / (1.0 + jnp.exp(-z))


def _inproj_kernel(x_ref, nw_ref, w_ref, cs_ref, h_ref, p_ref, vt_ref, *, vt_tile, vt_off):
    @pl.when(pl.program_id(1) == 0)
    def _():
        x = x_ref[...]
        ms = jnp.mean(x * x, axis=-1, keepdims=True)
        h_ref[...] = (x * lax.rsqrt(ms + EPS) * nw_ref[...]).astype(BF16)

    acc = jnp.dot(h_ref[...], w_ref[...], preferred_element_type=F32) * cs_ref[...]
    p_ref[...] = acc.astype(BF16)

    @pl.when(pl.program_id(1) == vt_tile)
    def _():
        vt_ref[...] = acc[:, vt_off:vt_off + BRANCH_W].T.astype(BF16)


def _inproj(x2, norm_w, w_in_bf, col_scale, tm=1024, tn=IN_COLS // 4):
    t = x2.shape[0]
    vt_tile, vt_off = divmod(COL_DF_V, tn)
    assert vt_off + BRANCH_W <= tn and vt_off % LANES == 0, "value columns must sit inside one column tile"
    return pl.pallas_call(
        functools.partial(_inproj_kernel, vt_tile=vt_tile, vt_off=vt_off),
        grid=(t // tm, IN_COLS // tn),
        in_specs=[
            pl.BlockSpec((tm, D_MODEL), lambda i, j: (i, 0)),
            pl.BlockSpec((1, D_MODEL), lambda i, j: (0, 0)),
            pl.BlockSpec((D_MODEL, tn), lambda i, j: (0, j)),
            pl.BlockSpec((1, tn), lambda i, j: (0, j)),
        ],
        out_specs=[
            pl.BlockSpec((tm, D_MODEL), lambda i, j: (i, 0)),
            pl.BlockSpec((tm, tn), lambda i, j: (i, j)),
            pl.BlockSpec((BRANCH_W, tm), lambda i, j: (0, i)),
        ],
        out_shape=[
            jax.ShapeDtypeStruct((t, D_MODEL), BF16),
            jax.ShapeDtypeStruct((t, IN_COLS), BF16),
            jax.ShapeDtypeStruct((BRANCH_W, t), BF16),
        ],
        compiler_params=_params(("parallel", "arbitrary")),
        name="inproj",
    )(x2, norm_w.reshape(1, D_MODEL), w_in_bf, col_scale)


NA_QROWS = 8
NA_KROWS = 16
NA_TQ = NA_QROWS * GRID_W
NA_TK = NA_KROWS * GRID_W
NA_KSPLIT = 4
NA_KBLK = NA_TK // NA_KSPLIT


def _na_window_start(rb, rows):
    lo = rb * NA_QROWS - NA_ROWS_MAX // 2
    return jnp.clip(lo, 0, rows - NA_KROWS)


def _na_patterns(rows):
    nrb = rows // NA_QROWS
    kr_win = min(NA_ROWS_MAX, rows)
    n_roff = 2 * NA_ROWS_MAX - 1
    n_coff = 2 * NA_COLS - 1
    pats, pat_ids = [], []
    for rb in range(nrb):
        r = rb * NA_QROWS + np.arange(NA_QROWS)
        start = int(np.clip(rb * NA_QROWS - NA_ROWS_MAX // 2, 0, rows - NA_KROWS))
        kr = start + np.arange(NA_KROWS)
        rs = np.clip(r - kr_win // 2, 0, rows - kr_win)
        ok = (kr[None, :] >= rs[:, None]) & (kr[None, :] < rs[:, None] + kr_win)
        off = kr[None, :] - r[:, None] + (NA_ROWS_MAX - 1)
        sel = np.zeros((NA_QROWS, NA_KROWS, n_roff + 1), np.float32)
        for i in range(NA_QROWS):
            for j in range(NA_KROWS):
                sel[i, j, off[i, j] if ok[i, j] else n_roff] = 1.0
        for pid, p in enumerate(pats):
            if np.array_equal(p, sel):
                pat_ids.append(pid)
                break
        else:
            pat_ids.append(len(pats))
            pats.append(sel)
    c = np.arange(GRID_W)
    cs = np.clip(c - NA_COLS // 2, 0, GRID_W - NA_COLS)
    col_sel = np.zeros((GRID_W, GRID_W, n_coff), np.float32)
    for qc in range(GRID_W):
        for kc in range(cs[qc], cs[qc] + NA_COLS):
            col_sel[qc, kc, kc - qc + (NA_COLS - 1)] = 1.0
    return pat_ids, np.stack(pats), col_sel


def _na_dense_bias(rel_bias, row_sel, col_sel):
    npat = row_sel.shape[0]
    cols = jnp.einsum("hab,qkb->haqk", rel_bias.astype(F32), col_sel, precision=lax.Precision.HIGHEST)
    cols = jnp.where((col_sel.sum(-1) > 0)[None, None], cols, MASK_VALUE)
    cols = jnp.concatenate([cols, jnp.full((NA_HEADS, 1, GRID_W, GRID_W), MASK_VALUE, F32)], axis=1)
    dense = None
    for a in range(row_sel.shape[-1]):
        term = row_sel[:, None, :, None, :, None, a] * cols[None, :, a, None, :, None, :]
        dense = term if dense is None else dense + term
    return dense.reshape(npat, NA_HEADS, NA_TQ, NA_TK)


def _na_kernel(q_ref, k0, k1, k2, k3, v0, v1, v2, v3, z_ref, b_ref, o_ref):
    k_refs = (k0, k1, k2, k3)
    v_refs = (v0, v1, v2, v3)
    nt = (((1,), (1,)), ((), ()))
    for h in range(NA_HEADS):
        hs = slice(h * NA_HEAD_DIM, (h + 1) * NA_HEAD_DIM)
        q = q_ref[:, hs]
        s = jnp.concatenate(
            [lax.dot_general(q, kr[:, hs], nt, preferred_element_type=F32) for kr in k_refs],
            axis=-1)
        s = s + b_ref[h]
        m = jnp.max(s, axis=-1, keepdims=True)
        p = jnp.exp(s - m)
        l = jnp.sum(p, axis=-1, keepdims=True)
        pb = p.astype(BF16)
        o = jnp.zeros((NA_TQ, NA_HEAD_DIM), F32)
        for j, vr in enumerate(v_refs):
            o = o + jnp.dot(pb[:, j * NA_KBLK:(j + 1) * NA_KBLK], vr[:, hs],
                            preferred_element_type=F32)
        o = o / l
        o_ref[:, hs] = (o * _silu(z_ref[:, hs].astype(F32))).astype(BF16)


def _na_attention(proj, dense_bias, pat_ids, batch, seq):
    rows = seq // GRID_W
    nrb = rows // NA_QROWS
    t = batch * seq
    qblk_per_seq = seq // NA_TQ
    kblk_per_seq = seq // NA_KBLK
    kblk_per_row = NA_KBLK // GRID_W

    def q_map(col):
        return lambda rb, b: (b * qblk_per_seq + rb, col)

    def kv_map(col, j):
        def f(rb, b):
            start = _na_window_start(rb, rows) // kblk_per_row
            return (b * kblk_per_seq + start + j, col)
        return f

    in_specs = [pl.BlockSpec((NA_TQ, BRANCH_W), q_map(COL_NA_Q // BRANCH_W))]
    in_specs += [pl.BlockSpec((NA_KBLK, BRANCH_W), kv_map(COL_NA_K // BRANCH_W, j)) for j in range(NA_KSPLIT)]
    in_specs += [pl.BlockSpec((NA_KBLK, BRANCH_W), kv_map(COL_NA_V // BRANCH_W, j)) for j in range(NA_KSPLIT)]
    in_specs += [pl.BlockSpec((NA_TQ, BRANCH_W), q_map(COL_Z // BRANCH_W + 0))]
    def bias_map(rb, b):
        pid = 0
        for r, p in enumerate(pat_ids):
            pid = pid + jnp.where(rb == r, p, 0)
        return (pid, 0, 0, 0)

    in_specs += [pl.BlockSpec((None, NA_HEADS, NA_TQ, NA_TK), bias_map)]
    return pl.pallas_call(
        _na_kernel,
        grid=(nrb, batch),
        in_specs=in_specs,
        out_specs=pl.BlockSpec((NA_TQ, BRANCH_W), lambda rb, b: (b * qblk_per_seq + rb, 0)),
        out_shape=jax.ShapeDtypeStruct((t, BRANCH_W), BF16),
        compiler_params=_params(("parallel", "parallel")),
        name="na_attention",
    )(*([proj] * 10), dense_bias)


DIFF_QCHUNK = 512
DIFF_KCHUNK = 1024
LOG2E = math.log2(math.e)


def _alibi_slopes():
    return [2.0 ** (-8.0 * (h + 1) / DIFF_HEADS) * LOG2E for h in range(DIFF_HEADS)]


def _split3(x):
    def trunc(v):
        return (v.view(np.uint32) & np.uint32(0xFFFF0000)).view(np.float32)
    x1 = trunc(x)
    r1 = x - x1
    x2 = trunc(r1)
    return x1, x2, r1 - x2


def _alibi_features(seq):
    pos = np.arange(seq, dtype=np.float32)
    slopes = np.asarray(_alibi_slopes(), np.float32)[:, None]
    u = _split3(-slopes * pos[None, :])
    w = _split3(slopes * pos[None, :])
    one = np.ones((DIFF_HEADS, seq), np.float32)
    zero = np.zeros((DIFF_HEADS, seq, LANES - 6), np.float32)
    qfeat = np.concatenate([np.stack(list(u) + [one] * 3, axis=-1), zero], axis=-1)
    kfeat = np.concatenate([np.stack([one] * 3 + list(w), axis=-1), zero], axis=-1)
    return jnp.asarray(qfeat).astype(BF16), jnp.asarray(kfeat).astype(BF16)


def _diff_kernel(lam_ref, sw_ref, q_ref, qf_ref, k_ref, kf_ref, vt_ref, corr_ref, z_ref, o_ref,
                 qa_ref, m_ref, l_ref, acc_ref, *, tq, tk, lambda_init):
    h = pl.program_id(1)
    qi = pl.program_id(2)
    kj = pl.program_id(3)
    nt = (((1,), (1,)), ((), ()))
    hw = 2 * DIFF_HEAD_DIM

    @pl.when(kj == 0)
    def _():
        m_ref[...] = jnp.full(m_ref.shape, MASK_VALUE, F32)
        l_ref[...] = jnp.zeros(l_ref.shape, F32)
        acc_ref[...] = jnp.zeros(acc_ref.shape, F32)
        q = q_ref[...]
        lane = lax.broadcasted_iota(jnp.int32, q.shape, 1)
        for c in range(2):
            in_c = (lane >= c * DIFF_HEAD_DIM) & (lane < (c + 1) * DIFF_HEAD_DIM)
            qa_ref[c, :, :hw] = jnp.where(in_c, q, jnp.zeros_like(q))
            qa_ref[c, :, hw:] = qf_ref[...]

    keys_after = kj > qi
    k = k_ref[...]
    kf = kf_ref[...]
    vt = vt_ref[...]

    def step(diag):
        if diag:
            ka_pos = jnp.concatenate([k, kf], axis=1)
            ka_neg = jnp.concatenate([k, -kf], axis=1)
        else:
            ka_any = jnp.concatenate([k, jnp.where(keys_after, -kf, kf)], axis=1)
        chunks = [(kk, c, n) for kk in range(tk // DIFF_KCHUNK) for c in range(2)
                  for n in range(tq // DIFF_QCHUNK)]

        def slices(kk, c, n):
            return (slice(kk * DIFF_KCHUNK, (kk + 1) * DIFF_KCHUNK),
                    slice(n * DIFF_QCHUNK, (n + 1) * DIFF_QCHUNK))

        def scores(kk, c, n):
            ks, cs = slices(kk, c, n)
            qa = qa_ref[c, cs, :]
            if not diag:
                return lax.dot_general(ka_any[ks], qa, nt, preferred_element_type=F32)
            k_lo, q_lo = kk * DIFF_KCHUNK, n * DIFF_QCHUNK
            if k_lo + DIFF_KCHUNK - 1 <= q_lo:
                return lax.dot_general(ka_pos[ks], qa, nt, preferred_element_type=F32)
            if k_lo >= q_lo + DIFF_QCHUNK - 1:
                return lax.dot_general(ka_neg[ks], qa, nt, preferred_element_type=F32)
            off = q_lo - k_lo
            st = lax.dot_general(ka_pos[ks], qa, nt, preferred_element_type=F32)
            return st + corr_ref[:, off:off + DIFF_QCHUNK]

        def softmax(st, kk, c, n):
            _, cs = slices(kk, c, n)
            m_prev = m_ref[c, :, cs]
            m_new = jnp.maximum(m_prev, jnp.max(st, axis=0, keepdims=True))
            alpha = jnp.exp2(m_prev - m_new)
            p = jnp.exp2(st - m_new)
            l_ref[c, :, cs] = alpha * l_ref[c, :, cs] + jnp.sum(p, axis=0, keepdims=True)
            m_ref[c, :, cs] = m_new
            return alpha, p.astype(BF16)

        def accumulate(alpha, p, kk, c, n):
            ks, cs = slices(kk, c, n)
            acc_ref[c, :, cs] = alpha * acc_ref[c, :, cs] + jnp.dot(vt[:, ks], p, preferred_element_type=F32)

        nc = len(chunks)
        st_q = {0: scores(*chunks[0])}
        if nc > 1:
            st_q[1] = scores(*chunks[1])
        p_q = {0: softmax(st_q.pop(0), *chunks[0])}
        for i in range(nc):
            if i + 2 < nc:
                st_q[i + 2] = scores(*chunks[i + 2])
            if i + 1 < nc:
                p_q[i + 1] = softmax(st_q.pop(i + 1), *chunks[i + 1])
            accumulate(*p_q.pop(i), *chunks[i])

    pl.when(qi != kj)(lambda: step(False))
    pl.when(qi == kj)(lambda: step(True))

    @pl.when(kj == pl.num_programs(3) - 1)
    def _():
        lv = lam_ref[...]
        lam = (jnp.exp(jnp.sum(lv[0:1] * lv[1:2], axis=-1, keepdims=True))
               - jnp.exp(jnp.sum(lv[2:3] * lv[3:4], axis=-1, keepdims=True)) + lambda_init)
        at = acc_ref[0] / l_ref[0] - lam * (acc_ref[1] / l_ref[1])
        ms = jnp.mean(at * at, axis=0, keepdims=True)
        y = (at * lax.rsqrt(ms + EPS)).T * sw_ref[...] * (1.0 - lambda_init)
        o_ref[...] = (y * _silu(z_ref[...].astype(F32))).astype(BF16)


def _diff_attention(proj, v_t, lam_vecs, subln_w, batch, seq, lambda_init, tq=2048, tk=2048):
    t = batch * seq
    nq = seq // tq
    nk = seq // tk
    hw = 2 * DIFF_HEAD_DIM
    assert tq == tk and DIFF_KCHUNK % DIFF_QCHUNK == 0, "diagonal handling assumes aligned square tiles"
    qfeat, kfeat = _alibi_features(seq)
    rel = jnp.arange(DIFF_KCHUNK, dtype=F32)[:, None] - jnp.arange(DIFF_KCHUNK, dtype=F32)[None, :]
    corr = -2.0 * jnp.asarray(_alibi_slopes(), F32)[:, None, None] * jnp.maximum(rel, 0.0)[None]
    kern = functools.partial(_diff_kernel, tq=tq, tk=tk, lambda_init=lambda_init)
    return pl.pallas_call(
        kern,
        grid=(batch, DIFF_HEADS, nq, nk),
        in_specs=[
            pl.BlockSpec((4, DIFF_HEAD_DIM), lambda b, h, i, j: (0, 0)),
            pl.BlockSpec((1, hw), lambda b, h, i, j: (0, 0)),
            pl.BlockSpec((tq, hw), lambda b, h, i, j: (b * nq + i, COL_DF_Q // hw + h)),
            pl.BlockSpec((None, tq, LANES), lambda b, h, i, j: (h, i, 0)),
            pl.BlockSpec((tk, hw), lambda b, h, i, j: (b * nk + j, COL_DF_K // hw + h)),
            pl.BlockSpec((None, tk, LANES), lambda b, h, i, j: (h, j, 0)),
            pl.BlockSpec((hw, tk), lambda b, h, i, j: (h, b * nk + j)),
            pl.BlockSpec((None, DIFF_KCHUNK, DIFF_KCHUNK), lambda b, h, i, j: (h, 0, 0)),
            pl.BlockSpec((tq, hw), lambda b, h, i, j: (b * nq + i, (COL_Z + BRANCH_W) // hw + h)),
        ],
        out_specs=pl.BlockSpec((tq, hw), lambda b, h, i, j: (b * nq + i, h)),
        out_shape=jax.ShapeDtypeStruct((t, BRANCH_W), BF16),
        scratch_shapes=[
            pltpu.VMEM((2, tq, 2 * hw), BF16),
            pltpu.VMEM((2, 1, tq), F32),
            pltpu.VMEM((2, 1, tq), F32),
            pltpu.VMEM((2, hw, tq), F32),
        ],
        compiler_params=_params(("parallel", "parallel", "parallel", "arbitrary")),
        name="diff_attention",
    )(lam_vecs, subln_w.reshape(1, hw), proj, qfeat, proj, kfeat, v_t, corr, proj)


def _pool_kernel(x_ref, z_ref, w_ref, sc_ref, o_ref, *, seq):
    t = lax.broadcasted_iota(jnp.int32, (seq, POOL_GROUP_W), 0)
    for g, win in enumerate(POOL_WINDOWS):
        gs = slice(g * POOL_GROUP_W, (g + 1) * POOL_GROUP_W)
        x = x_ref[:, gs].astype(F32)
        half = win // 2
        total = x
        for d in range(-half, win - half):
            if d == 0:
                continue
            shifted = pltpu.roll(x, (-d) % seq, axis=0)
            ok = (t + d >= 0) & (t + d < seq)
            total = total + jnp.where(ok, shifted, 0.0)
        cnt = jnp.minimum(t + (win - half), seq) - jnp.maximum(t - half, 0)
        pooled = total / cnt.astype(F32) - x
        y = jnp.dot(pooled.astype(BF16), w_ref[g], preferred_element_type=F32) * sc_ref[:, gs]
        o_ref[:, gs] = (y * _silu(z_ref[:, gs].astype(F32))).astype(BF16)


def _pool_mixer(proj, pool_w_bf, pool_scale, batch, seq):
    t = batch * seq
    return pl.pallas_call(
        functools.partial(_pool_kernel, seq=seq),
        grid=(batch,),
        in_specs=[
            pl.BlockSpec((seq, BRANCH_W), lambda b: (b, COL_POOL // BRANCH_W)),
            pl.BlockSpec((seq, BRANCH_W), lambda b: (b, COL_Z // BRANCH_W + 2)),
            pl.BlockSpec((len(POOL_WINDOWS), POOL_GROUP_W, POOL_GROUP_W), lambda b: (0, 0, 0)),
            pl.BlockSpec((1, BRANCH_W), lambda b: (0, 0)),
        ],
        out_specs=pl.BlockSpec((seq, BRANCH_W), lambda b: (b, 0)),
        out_shape=jax.ShapeDtypeStruct((t, BRANCH_W), BF16),
        compiler_params=_params(("parallel",)),
        name="pool_mixer",
    )(proj, proj, pool_w_bf, pool_scale.reshape(1, BRANCH_W))


def _sgu_kernel(u_ref, v_ref, z_ref, lw_ref, lb_ref, ws_ref, bs_ref, o_ref, *, tm):
    v = v_ref[...].astype(F32)
    mu = jnp.mean(v, axis=-1, keepdims=True)
    vc = v - mu
    var = jnp.mean(vc * vc, axis=-1, keepdims=True)
    vn = (vc * lax.rsqrt(var + EPS) * lw_ref[...] + lb_ref[...]).astype(BF16)
    for n in range(tm // SGU_CHUNK):
        rs = slice(n * SGU_CHUNK, (n + 1) * SGU_CHUNK)
        for g in range(SGU_GROUPS):
            gs = slice(g * SGU_GROUP_W, (g + 1) * SGU_GROUP_W)
            mixed = jnp.dot(ws_ref[g], vn[rs, gs], preferred_element_type=F32) + bs_ref[g]
            y = u_ref[rs, gs].astype(F32) * mixed
            o_ref[rs, gs] = (y * _silu(z_ref[rs, gs].astype(F32))).astype(BF16)


def _sgu_mixer(proj, ln_w, ln_b, ws_bf, bs_full, t, tm=512):
    return pl.pallas_call(
        functools.partial(_sgu_kernel, tm=tm),
        grid=(t // tm,),
        in_specs=[
            pl.BlockSpec((tm, BRANCH_W), lambda i: (i, COL_SGU_U // BRANCH_W)),
            pl.BlockSpec((tm, BRANCH_W), lambda i: (i, COL_SGU_V // BRANCH_W)),
            pl.BlockSpec((tm, BRANCH_W), lambda i: (i, COL_Z // BRANCH_W + 3)),
            pl.BlockSpec((1, BRANCH_W), lambda i: (0, 0)),
            pl.BlockSpec((1, BRANCH_W), lambda i: (0, 0)),
            pl.BlockSpec((SGU_GROUPS, SGU_CHUNK, SGU_CHUNK), lambda i: (0, 0, 0)),
            pl.BlockSpec((SGU_GROUPS, SGU_CHUNK, SGU_GROUP_W), lambda i: (0, 0, 0)),
        ],
        out_specs=pl.BlockSpec((tm, BRANCH_W), lambda i: (i, 0)),
        out_shape=jax.ShapeDtypeStruct((t, BRANCH_W), BF16),
        compiler_params=_params(("parallel",)),
        name="sgu_mixer",
    )(proj, proj, proj, ln_w.reshape(1, BRANCH_W), ln_b.reshape(1, BRANCH_W), ws_bf, bs_full)


def _merge_kernel(h_ref, ya_ref, yb_ref, yc_ref, yd_ref, wg_ref, wb_ref, o_ref):
    h = h_ref[...]
    merged = None
    for i, y_ref in enumerate((ya_ref, yb_ref, yc_ref, yd_ref)):
        gate = _sigmoid(jnp.dot(h, wg_ref[i], preferred_element_type=F32))
        term = gate * jnp.dot(y_ref[...], wb_ref[i], preferred_element_type=F32)
        merged = term if merged is None else merged + term
    o_ref[...] = merged.astype(BF16)


def _merge(h, ys, wg_bf, wb_bf, tm=1024, tn=512):
    t = h.shape[0]
    y_spec = pl.BlockSpec((tm, BRANCH_W), lambda i, j: (i, 0))
    return pl.pallas_call(
        _merge_kernel,
        grid=(t // tm, D_MODEL // tn),
        in_specs=[
            pl.BlockSpec((tm, D_MODEL), lambda i, j: (i, 0)),
            y_spec, y_spec, y_spec, y_spec,
            pl.BlockSpec((N_BRANCH, D_MODEL, tn), lambda i, j: (0, 0, j)),
            pl.BlockSpec((N_BRANCH, BRANCH_W, tn), lambda i, j: (0, 0, j)),
        ],
        out_specs=pl.BlockSpec((tm, tn), lambda i, j: (i, j)),
        out_shape=jax.ShapeDtypeStruct((t, D_MODEL), BF16),
        compiler_params=_params(("parallel", "arbitrary")),
        name="merge",
    )(h, *ys, wg_bf, wb_bf)


def _outproj_kernel(x_ref, m_ref, w_ref, fw_ref, o_ref, *, final_norm):
    x = x_ref[...] + jnp.dot(m_ref[...], w_ref[...], preferred_element_type=F32)
    if final_norm:
        ms = jnp.mean(x * x, axis=-1, keepdims=True)
        x = x * lax.rsqrt(ms + EPS) * fw_ref[...]
    o_ref[...] = x


def _outproj(x2, merged, w_out_bf, final_w, final_norm, tm=512):
    t = x2.shape[0]
    return pl.pallas_call(
        functools.partial(_outproj_kernel, final_norm=final_norm),
        grid=(t // tm,),
        in_specs=[
            pl.BlockSpec((tm, D_MODEL), lambda i: (i, 0)),
            pl.BlockSpec((tm, D_MODEL), lambda i: (i, 0)),
            pl.BlockSpec((D_MODEL, D_MODEL), lambda i: (0, 0)),
            pl.BlockSpec((1, D_MODEL), lambda i: (0, 0)),
        ],
        out_specs=pl.BlockSpec((tm, D_MODEL), lambda i: (i, 0)),
        out_shape=jax.ShapeDtypeStruct((t, D_MODEL), F32),
        compiler_params=_params(("parallel",)),
        name="outproj",
    )(x2, merged, w_out_bf, final_w.reshape(1, D_MODEL))


def _column_scale():
    cs = jnp.ones((IN_COLS,), F32)
    cs = cs.at[COL_NA_Q:COL_NA_Q + BRANCH_W].set(NA_HEAD_DIM ** -0.5)
    cs = cs.at[COL_DF_Q:COL_DF_Q + BRANCH_W].set(DIFF_HEAD_DIM ** -0.5 * LOG2E)
    return cs.reshape(1, IN_COLS)


def kernel(x, norm_w, w_in, na_bias, lam_q1, lam_k1, lam_q2, lam_k2, diff_subln_w, pool_w, pool_scale,
           sgu_ln_w, sgu_ln_b, sgu_ws, sgu_b, w_branch, w_gate, w_out, final_norm_w):
    batch, seq, _ = x.shape
    depth = norm_w.shape[0]
    t = batch * seq
    rows = seq // GRID_W
    x2 = x.reshape(t, D_MODEL)
    col_scale = _column_scale()
    pat_ids, row_sel, col_sel = _na_patterns(rows)
    for l in range(depth):
        h, proj, v_t = _inproj(x2, norm_w[l], w_in[l].astype(BF16), col_scale)
        y_a = _na_attention(proj, _na_dense_bias(na_bias[l], row_sel, col_sel), pat_ids, batch, seq)
        lambda_init = 0.8 - 0.6 * math.exp(-0.3 * l)
        lam_vecs = jnp.stack([lam_q1[l], lam_k1[l], lam_q2[l], lam_k2[l]]).astype(F32)
        y_b = _diff_attention(proj, v_t, lam_vecs, diff_subln_w[l], batch, seq, lambda_init)
        y_c = _pool_mixer(proj, pool_w[l].astype(BF16), pool_scale[l], batch, seq)
        bs_full = jnp.broadcast_to(sgu_b[l][:, :, None], (SGU_GROUPS, SGU_CHUNK, SGU_GROUP_W))
        y_d = _sgu_mixer(proj, sgu_ln_w[l], sgu_ln_b[l], sgu_ws[l].astype(BF16), bs_full, t)
        merged = _merge(h, (y_a, y_b, y_c, y_d), w_gate[l].astype(BF16), w_branch[l].astype(BF16))
        x2 = _outproj(x2, merged, w_out[l].astype(BF16), final_norm_w, l == depth - 1)
    return x2.reshape(batch, seq, D_MODEL)
```

```python
import functools
import math

import jax
import jax.numpy as jnp
import numpy as np
from jax import lax
from jax.experimental import pallas as pl
from jax.experimental.pallas import tpu as pltpu

D_MODEL = 2048
GRID_W = 64
N_BRANCH = 4
BRANCH_W = D_MODEL // N_BRANCH
NA_HEADS = 4
NA_HEAD_DIM = BRANCH_W // NA_HEADS
NA_ROWS_MAX = 8
NA_COLS = 16
DIFF_HEADS = 4
DIFF_HEAD_DIM = BRANCH_W // (2 * DIFF_HEADS)
POOL_WINDOWS = (2, 4, 8, 16)
POOL_GROUP_W = BRANCH_W // len(POOL_WINDOWS)
SGU_CHUNK = 128
SGU_GROUPS = 4
SGU_GROUP_W = BRANCH_W // SGU_GROUPS
IN_COLS = 13 * BRANCH_W
EPS = 1e-6

COL_NA_Q = 0
COL_NA_K = BRANCH_W
COL_NA_V = 2 * BRANCH_W
COL_DF_Q = 3 * BRANCH_W
COL_DF_K = 4 * BRANCH_W
COL_DF_V = 5 * BRANCH_W
COL_POOL = 6 * BRANCH_W
COL_SGU_U = 7 * BRANCH_W
COL_SGU_V = 8 * BRANCH_W
COL_Z = 9 * BRANCH_W

LANES = 128
VMEM_LIMIT = 56 * 1024 * 1024
MASK_VALUE = -1e30

BF16 = jnp.bfloat16
F32 = jnp.float32


def _params(sem, vmem=VMEM_LIMIT):
    return pltpu.CompilerParams(dimension_semantics=sem, vmem_limit_bytes=vmem)


def _silu(z):
    return z * (1.0 / (1.0 + jnp.exp(-z)))


def _sigmoid(z):
    return 1.0 / (1.0 + jnp.exp(-z))


def _inproj_kernel(x_ref, nw_ref, w_ref, cs_ref, h_ref, p_ref, vt_ref, *, vt_tile, vt_off):
    @pl.when(pl.program_id(1) == 0)
    def _():
        x = x_ref[...]
        ms = jnp.mean(x * x, axis=-1, keepdims=True)
        h_ref[...] = (x * lax.rsqrt(ms + EPS) * nw_ref[...]).astype(BF16)

    acc = jnp.dot(h_ref[...], w_ref[...], preferred_element_type=F32) * cs_ref[...]
    p_ref[...] = acc.astype(BF16)

    @pl.when(pl.program_id(1) == vt_tile)
    def _():
        vt_ref[...] = acc[:, vt_off:vt_off + BRANCH_W].T.astype(BF16)


def _inproj(x2, norm_w, w_in_bf, col_scale, tm=1024, tn=IN_COLS // 4):
    t = x2.shape[0]
    vt_tile, vt_off = divmod(COL_DF_V, tn)
    assert vt_off + BRANCH_W <= tn and vt_off % LANES == 0, "value columns must sit inside one column tile"
    return pl.pallas_call(
        functools.partial(_inproj_kernel, vt_tile=vt_tile, vt_off=vt_off),
        grid=(t // tm, IN_COLS // tn),
        in_specs=[
            pl.BlockSpec((tm, D_MODEL), lambda i, j: (i, 0)),
            pl.BlockSpec((1, D_MODEL), lambda i, j: (0, 0)),
            pl.BlockSpec((D_MODEL, tn), lambda i, j: (0, j)),
            pl.BlockSpec((1, tn), lambda i, j: (0, j)),
        ],
        out_specs=[
            pl.BlockSpec((tm, D_MODEL), lambda i, j: (i, 0)),
            pl.BlockSpec((tm, tn), lambda i, j: (i, j)),
            pl.BlockSpec((BRANCH_W, tm), lambda i, j: (0, i)),
        ],
        out_shape=[
            jax.ShapeDtypeStruct((t, D_MODEL), BF16),
            jax.ShapeDtypeStruct((t, IN_COLS), BF16),
            jax.ShapeDtypeStruct((BRANCH_W, t), BF16),
        ],
        compiler_params=_params(("parallel", "arbitrary")),
        name="inproj",
    )(x2, norm_w.reshape(1, D_MODEL), w_in_bf, col_scale)


NA_QROWS = 8
NA_KROWS = 16
NA_TQ = NA_QROWS * GRID_W
NA_TK = NA_KROWS * GRID_W
NA_KSPLIT = 4
NA_KBLK = NA_TK // NA_KSPLIT


def _na_window_start(rb, rows):
    lo = rb * NA_QROWS - NA_ROWS_MAX // 2
    return jnp.clip(lo, 0, rows - NA_KROWS)


def _na_patterns(rows):
    nrb = rows // NA_QROWS
    kr_win = min(NA_ROWS_MAX, rows)
    n_roff = 2 * NA_ROWS_MAX - 1
    n_coff = 2 * NA_COLS - 1
    pats, pat_ids = [], []
    for rb in range(nrb):
        r = rb * NA_QROWS + np.arange(NA_QROWS)
        start = int(np.clip(rb * NA_QROWS - NA_ROWS_MAX // 2, 0, rows - NA_KROWS))
        kr = start + np.arange(NA_KROWS)
        rs = np.clip(r - kr_win // 2, 0, rows - kr_win)
        ok = (kr[None, :] >= rs[:, None]) & (kr[None, :] < rs[:, None] + kr_win)
        off = kr[None, :] - r[:, None] + (NA_ROWS_MAX - 1)
        sel = np.zeros((NA_QROWS, NA_KROWS, n_roff + 1), np.float32)
        for i in range(NA_QROWS):
            for j in range(NA_KROWS):
                sel[i, j, off[i, j] if ok[i, j] else n_roff] = 1.0
        for pid, p in enumerate(pats):
            if np.array_equal(p, sel):
                pat_ids.append(pid)
                break
        else:
            pat_ids.append(len(pats))
            pats.append(sel)
    c = np.arange(GRID_W)
    cs = np.clip(c - NA_COLS // 2, 0, GRID_W - NA_COLS)
    col_sel = np.zeros((GRID_W, GRID_W, n_coff), np.float32)
    for qc in range(GRID_W):
        for kc in range(cs[qc], cs[qc] + NA_COLS):
            col_sel[qc, kc, kc - qc + (NA_COLS - 1)] = 1.0
    return pat_ids, np.stack(pats), col_sel


def _na_dense_bias(rel_bias, row_sel, col_sel):
    npat = row_sel.shape[0]
    n_sel = row_sel.shape[-1]
    cols = jnp.einsum("hab,qkb->haqk", rel_bias.astype(F32), col_sel, precision=lax.Precision.HIGHEST)
    cols = jnp.where((col_sel.sum(-1) > 0)[None, None], cols, MASK_VALUE)
    cols = jnp.concatenate([cols, jnp.full((NA_HEADS, 1, GRID_W, GRID_W), MASK_VALUE, F32)], axis=1)
    cols2 = jnp.concatenate([cols, cols], axis=-1)
    tile_ids = jnp.asarray(np.argmax(row_sel, axis=-1).reshape(-1), jnp.int32)

    def body(ids_ref, cols_ref, o_ref):
        base = (pl.program_id(0) * NA_QROWS + pl.program_id(2)) * NA_KROWS
        left = lax.broadcasted_iota(jnp.int32, (GRID_W, 2 * GRID_W), 1) < GRID_W
        for jj in range(NA_KROWS // 2):
            even = cols_ref[ids_ref[base + 2 * jj]]
            odd = cols_ref[ids_ref[base + 2 * jj + 1]]
            o_ref[:, jj * 2 * GRID_W:(jj + 1) * 2 * GRID_W] = jnp.where(left, even, odd)

    return pl.pallas_call(
        body,
        grid_spec=pltpu.PrefetchScalarGridSpec(
            num_scalar_prefetch=1,
            grid=(npat, NA_HEADS, NA_QROWS),
            in_specs=[pl.BlockSpec((None, n_sel, GRID_W, 2 * GRID_W), lambda p, h, i, ids: (h, 0, 0, 0))],
            out_specs=pl.BlockSpec((None, None, GRID_W, NA_TK), lambda p, h, i, ids: (p, h, i, 0)),
        ),
        out_shape=jax.ShapeDtypeStruct((npat, NA_HEADS, NA_TQ, NA_TK), F32),
        compiler_params=_params(("parallel", "parallel", "parallel")),
        name="na_bias_expand",
    )(tile_ids, cols2)


def _na_kernel(q_ref, k0, k1, k2, k3, v0, v1, v2, v3, z_ref, b_ref, o_ref):
    k_refs = (k0, k1, k2, k3)
    v_refs = (v0, v1, v2, v3)
    nt = (((1,), (1,)), ((), ()))
    for h in range(NA_HEADS):
        hs = slice(h * NA_HEAD_DIM, (h + 1) * NA_HEAD_DIM)
        q = q_ref[:, hs]
        s = jnp.concatenate(
            [lax.dot_general(q, kr[:, hs], nt, preferred_element_type=F32) for kr in k_refs],
            axis=-1)
        s = s + b_ref[h]
        m = jnp.max(s, axis=-1, keepdims=True)
        p = jnp.exp(s - m)
        l = jnp.sum(p, axis=-1, keepdims=True)
        pb = p.astype(BF16)
        o = jnp.zeros((NA_TQ, NA_HEAD_DIM), F32)
        for j, vr in enumerate(v_refs):
            o = o + jnp.dot(pb[:, j * NA_KBLK:(j + 1) * NA_KBLK], vr[:, hs],
                            preferred_element_type=F32)
        o = o / l
        o_ref[:, hs] = (o * _silu(z_ref[:, hs].astype(F32))).astype(BF16)


def _na_attention(proj, dense_bias, pat_ids, batch, seq):
    rows = seq // GRID_W
    nrb = rows // NA_QROWS
    t = batch * seq
    qblk_per_seq = seq // NA_TQ
    kblk_per_seq = seq // NA_KBLK
    kblk_per_row = NA_KBLK // GRID_W

    def q_map(col):
        return lambda rb, b: (b * qblk_per_seq + rb, col)

    def kv_map(col, j):
        def f(rb, b):
            start = _na_window_start(rb, rows) // kblk_per_row
            return (b * kblk_per_seq + start + j, col)
        return f

    in_specs = [pl.BlockSpec((NA_TQ, BRANCH_W), q_map(COL_NA_Q // BRANCH_W))]
    in_specs += [pl.BlockSpec((NA_KBLK, BRANCH_W), kv_map(COL_NA_K // BRANCH_W, j)) for j in range(NA_KSPLIT)]
    in_specs += [pl.BlockSpec((NA_KBLK, BRANCH_W), kv_map(COL_NA_V // BRANCH_W, j)) for j in range(NA_KSPLIT)]
    in_specs += [pl.BlockSpec((NA_TQ, BRANCH_W), q_map(COL_Z // BRANCH_W + 0))]
    def bias_map(rb, b):
        pid = 0
        for r, p in enumerate(pat_ids):
            pid = pid + jnp.where(rb == r, p, 0)
        return (pid, 0, 0, 0)

    in_specs += [pl.BlockSpec((None, NA_HEADS, NA_TQ, NA_TK), bias_map)]
    return pl.pallas_call(
        _na_kernel,
        grid=(nrb, batch),
        in_specs=in_specs,
        out_specs=pl.BlockSpec((NA_TQ, BRANCH_W), lambda rb, b: (b * qblk_per_seq + rb, 0)),
        out_shape=jax.ShapeDtypeStruct((t, BRANCH_W), BF16),
        compiler_params=_params(("parallel", "parallel")),
        name="na_attention",
    )(*([proj] * 10), dense_bias)


DIFF_QCHUNK = 512
DIFF_KCHUNK = 1024
LOG2E = math.log2(math.e)


def _alibi_slopes():
    return [2.0 ** (-8.0 * (h + 1) / DIFF_HEADS) * LOG2E for h in range(DIFF_HEADS)]


def _split3(x):
    def trunc(v):
        return (v.view(np.uint32) & np.uint32(0xFFFF0000)).view(np.float32)
    x1 = trunc(x)
    r1 = x - x1
    x2 = trunc(r1)
    return x1, x2, r1 - x2


def _alibi_features(seq):
    pos = np.arange(seq, dtype=np.float32)
    slopes = np.asarray(_alibi_slopes(), np.float32)[:, None]
    u = _split3(-slopes * pos[None, :])
    w = _split3(slopes * pos[None, :])
    one = np.ones((DIFF_HEADS, seq), np.float32)
    zero = np.zeros((DIFF_HEADS, seq, LANES - 6), np.float32)
    qfeat = np.concatenate([np.stack(list(u) + [one] * 3, axis=-1), zero], axis=-1)
    kfeat = np.concatenate([np.stack([one] * 3 + list(w), axis=-1), zero], axis=-1)
    return jnp.asarray(qfeat).astype(BF16), jnp.asarray(kfeat).astype(BF16)


def _diff_kernel(lam_ref, sw_ref, q_ref, qf_ref, k_ref, kf_ref, vt_ref, corr_ref, z_ref, o_ref,
                 qa_ref, m_ref, l_ref, acc_ref, *, tq, tk, lambda_init):
    h = pl.program_id(1)
    qi = pl.program_id(2)
    kj = pl.program_id(3)
    nt = (((1,), (1,)), ((), ()))
    hw = 2 * DIFF_HEAD_DIM

    @pl.when(kj == 0)
    def _():
        m_ref[...] = jnp.full(m_ref.shape, MASK_VALUE, F32)
        l_ref[...] = jnp.zeros(l_ref.shape, F32)
        acc_ref[...] = jnp.zeros(acc_ref.shape, F32)
        q = q_ref[...]
        lane = lax.broadcasted_iota(jnp.int32, q.shape, 1)
        for c in range(2):
            in_c = (lane >= c * DIFF_HEAD_DIM) & (lane < (c + 1) * DIFF_HEAD_DIM)
            qa_ref[c, :, :hw] = jnp.where(in_c, q, jnp.zeros_like(q))
            qa_ref[c, :, hw:] = qf_ref[...]

    keys_after = kj > qi
    k = k_ref[...]
    kf = kf_ref[...]
    vt = vt_ref[...]

    def step(diag):
        if diag:
            ka_pos = jnp.concatenate([k, kf], axis=1)
            ka_neg = jnp.concatenate([k, -kf], axis=1)
        else:
            ka_any = jnp.concatenate([k, jnp.where(keys_after, -kf, kf)], axis=1)
        chunks = [(kk, c, n) for kk in range(tk // DIFF_KCHUNK) for c in range(2)
                  for n in range(tq // DIFF_QCHUNK)]

        def slices(kk, c, n):
            return (slice(kk * DIFF_KCHUNK, (kk + 1) * DIFF_KCHUNK),
                    slice(n * DIFF_QCHUNK, (n + 1) * DIFF_QCHUNK))

        def scores(kk, c, n):
            ks, cs = slices(kk, c, n)
            qa = qa_ref[c, cs, :]
            if not diag:
                return lax.dot_general(ka_any[ks], qa, nt, preferred_element_type=F32)
            k_lo, q_lo = kk * DIFF_KCHUNK, n * DIFF_QCHUNK
            if k_lo + DIFF_KCHUNK - 1 <= q_lo:
                return lax.dot_general(ka_pos[ks], qa, nt, preferred_element_type=F32)
            if k_lo >= q_lo + DIFF_QCHUNK - 1:
                return lax.dot_general(ka_neg[ks], qa, nt, preferred_element_type=F32)
            off = q_lo - k_lo
            st = lax.dot_general(ka_pos[ks], qa, nt, preferred_element_type=F32)
            return st + corr_ref[:, off:off + DIFF_QCHUNK]

        def softmax(st, kk, c, n):
            _, cs = slices(kk, c, n)
            m_prev = m_ref[c, :, cs]
            m_new = jnp.maximum(m_prev, jnp.max(st, axis=0, keepdims=True))
            alpha = jnp.exp2(m_prev - m_new)
            p = jnp.exp2(st - m_new)
            l_ref[c, :, cs] = alpha * l_ref[c, :, cs] + jnp.sum(p, axis=0, keepdims=True)
            m_ref[c, :, cs] = m_new
            return alpha, p.astype(BF16)

        def accumulate(alpha, p, kk, c, n):
            ks, cs = slices(kk, c, n)
            acc_ref[c, :, cs] = alpha * acc_ref[c, :, cs] + jnp.dot(vt[:, ks], p, preferred_element_type=F32)

        nc = len(chunks)
        st_q = {0: scores(*chunks[0])}
        if nc > 1:
            st_q[1] = scores(*chunks[1])
        p_q = {0: softmax(st_q.pop(0), *chunks[0])}
        for i in range(nc):
            if i + 2 < nc:
                st_q[i + 2] = scores(*chunks[i + 2])
            if i + 1 < nc:
                p_q[i + 1] = softmax(st_q.pop(i + 1), *chunks[i + 1])
            accumulate(*p_q.pop(i), *chunks[i])

    pl.when(qi != kj)(lambda: step(False))
    pl.when(qi == kj)(lambda: step(True))

    @pl.when(kj == pl.num_programs(3) - 1)
    def _():
        lv = lam_ref[...]
        lam = (jnp.exp(jnp.sum(lv[0:1] * lv[1:2], axis=-1, keepdims=True))
               - jnp.exp(jnp.sum(lv[2:3] * lv[3:4], axis=-1, keepdims=True)) + lambda_init)
        at = acc_ref[0] / l_ref[0] - lam * (acc_ref[1] / l_ref[1])
        ms = jnp.mean(at * at, axis=0, keepdims=True)
        y = (at * lax.rsqrt(ms + EPS)).T * sw_ref[...] * (1.0 - lambda_init)
        o_ref[...] = (y * _silu(z_ref[...].astype(F32))).astype(BF16)


def _diff_attention(proj, v_t, lam_vecs, subln_w, batch, seq, lambda_init, tq=2048, tk=2048):
    t = batch * seq
    nq = seq // tq
    nk = seq // tk
    hw = 2 * DIFF_HEAD_DIM
    assert tq == tk and DIFF_KCHUNK % DIFF_QCHUNK == 0, "diagonal handling assumes aligned square tiles"
    qfeat, kfeat = _alibi_features(seq)
    rel = jnp.arange(DIFF_KCHUNK, dtype=F32)[:, None] - jnp.arange(DIFF_KCHUNK, dtype=F32)[None, :]
    corr = -2.0 * jnp.asarray(_alibi_slopes(), F32)[:, None, None] * jnp.maximum(rel, 0.0)[None]
    kern = functools.partial(_diff_kernel, tq=tq, tk=tk, lambda_init=lambda_init)
    return pl.pallas_call(
        kern,
        grid=(batch, DIFF_HEADS, nq, nk),
        in_specs=[
            pl.BlockSpec((4, DIFF_HEAD_DIM), lambda b, h, i, j: (0, 0)),
            pl.BlockSpec((1, hw), lambda b, h, i, j: (0, 0)),
            pl.BlockSpec((tq, hw), lambda b, h, i, j: (b * nq + i, COL_DF_Q // hw + h)),
            pl.BlockSpec((None, tq, LANES), lambda b, h, i, j: (h, i, 0)),
            pl.BlockSpec((tk, hw), lambda b, h, i, j: (b * nk + j, COL_DF_K // hw + h)),
            pl.BlockSpec((None, tk, LANES), lambda b, h, i, j: (h, j, 0)),
            pl.BlockSpec((hw, tk), lambda b, h, i, j: (h, b * nk + j)),
            pl.BlockSpec((None, DIFF_KCHUNK, DIFF_KCHUNK), lambda b, h, i, j: (h, 0, 0)),
            pl.BlockSpec((tq, hw), lambda b, h, i, j: (b * nq + i, (COL_Z + BRANCH_W) // hw + h)),
        ],
        out_specs=pl.BlockSpec((tq, hw), lambda b, h, i, j: (b * nq + i, h)),
        out_shape=jax.ShapeDtypeStruct((t, BRANCH_W), BF16),
        scratch_shapes=[
            pltpu.VMEM((2, tq, 2 * hw), BF16),
            pltpu.VMEM((2, 1, tq), F32),
            pltpu.VMEM((2, 1, tq), F32),
            pltpu.VMEM((2, hw, tq), F32),
        ],
        compiler_params=_params(("parallel", "parallel", "parallel", "arbitrary")),
        name="diff_attention",
    )(lam_vecs, subln_w.reshape(1, hw), proj, qfeat, proj, kfeat, v_t, corr, proj)


def _pool_kernel(x_ref, z_ref, w_ref, sc_ref, o_ref, *, seq):
    t = lax.broadcasted_iota(jnp.int32, (seq, POOL_GROUP_W), 0)
    for g, win in enumerate(POOL_WINDOWS):
        gs = slice(g * POOL_GROUP_W, (g + 1) * POOL_GROUP_W)
        x = x_ref[:, gs].astype(F32)
        half = win // 2
        total = x
        for d in range(-half, win - half):
            if d == 0:
                continue
            shifted = pltpu.roll(x, (-d) % seq, axis=0)
            ok = (t + d >= 0) & (t + d < seq)
            total = total + jnp.where(ok, shifted, 0.0)
        cnt = jnp.minimum(t + (win - half), seq) - jnp.maximum(t - half, 0)
        pooled = total / cnt.astype(F32) - x
        y = jnp.dot(pooled.astype(BF16), w_ref[g], preferred_element_type=F32) * sc_ref[:, gs]
        o_ref[:, gs] = (y * _silu(z_ref[:, gs].astype(F32))).astype(BF16)


def _pool_mixer(proj, pool_w_bf, pool_scale, batch, seq):
    t = batch * seq
    return pl.pallas_call(
        functools.partial(_pool_kernel, seq=seq),
        grid=(batch,),
        in_specs=[
            pl.BlockSpec((seq, BRANCH_W), lambda b: (b, COL_POOL // BRANCH_W)),
            pl.BlockSpec((seq, BRANCH_W), lambda b: (b, COL_Z // BRANCH_W + 2)),
            pl.BlockSpec((len(POOL_WINDOWS), POOL_GROUP_W, POOL_GROUP_W), lambda b: (0, 0, 0)),
            pl.BlockSpec((1, BRANCH_W), lambda b: (0, 0)),
        ],
        out_specs=pl.BlockSpec((seq, BRANCH_W), lambda b: (b, 0)),
        out_shape=jax.ShapeDtypeStruct((t, BRANCH_W), BF16),
        compiler_params=_params(("parallel",)),
        name="pool_mixer",
    )(proj, proj, pool_w_bf, pool_scale.reshape(1, BRANCH_W))


def _sgu_kernel(u_ref, v_ref, z_ref, lw_ref, lb_ref, ws_ref, bs_ref, o_ref, *, tm):
    v = v_ref[...].astype(F32)
    mu = jnp.mean(v, axis=-1, keepdims=True)
    vc = v - mu
    var = jnp.mean(vc * vc, axis=-1, keepdims=True)
    vn = (vc * lax.rsqrt(var + EPS) * lw_ref[...] + lb_ref[...]).astype(BF16)
    for n in range(tm // SGU_CHUNK):
        rs = slice(n * SGU_CHUNK, (n + 1) * SGU_CHUNK)
        for g in range(SGU_GROUPS):
            gs = slice(g * SGU_GROUP_W, (g + 1) * SGU_GROUP_W)
            mixed = jnp.dot(ws_ref[g], vn[rs, gs], preferred_element_type=F32) + bs_ref[g]
            y = u_ref[rs, gs].astype(F32) * mixed
            o_ref[rs, gs] = (y * _silu(z_ref[rs, gs].astype(F32))).astype(BF16)


def _sgu_mixer(proj, ln_w, ln_b, ws_bf, bs_full, t, tm=512):
    return pl.pallas_call(
        functools.partial(_sgu_kernel, tm=tm),
        grid=(t // tm,),
        in_specs=[
            pl.BlockSpec((tm, BRANCH_W), lambda i: (i, COL_SGU_U // BRANCH_W)),
            pl.BlockSpec((tm, BRANCH_W), lambda i: (i, COL_SGU_V // BRANCH_W)),
            pl.BlockSpec((tm, BRANCH_W), lambda i: (i, COL_Z // BRANCH_W + 3)),
            pl.BlockSpec((1, BRANCH_W), lambda i: (0, 0)),
            pl.BlockSpec((1, BRANCH_W), lambda i: (0, 0)),
            pl.BlockSpec((SGU_GROUPS, SGU_CHUNK, SGU_CHUNK), lambda i: (0, 0, 0)),
            pl.BlockSpec((SGU_GROUPS, SGU_CHUNK, SGU_GROUP_W), lambda i: (0, 0, 0)),
        ],
        out_specs=pl.BlockSpec((tm, BRANCH_W), lambda i: (i, 0)),
        out_shape=jax.ShapeDtypeStruct((t, BRANCH_W), BF16),
        compiler_params=_params(("parallel",)),
        name="sgu_mixer",
    )(proj, proj, proj, ln_w.reshape(1, BRANCH_W), ln_b.reshape(1, BRANCH_W), ws_bf, bs_full)


def _merge_kernel(h_ref, ya_ref, yb_ref, yc_ref, yd_ref, wg_ref, wb_ref, o_ref):
    h = h_ref[...]
    merged = None
    for i, y_ref in enumerate((ya_ref, yb_ref, yc_ref, yd_ref)):
        gate = _sigmoid(jnp.dot(h, wg_ref[i], preferred_element_type=F32))
        term = gate * jnp.dot(y_ref[...], wb_ref[i], preferred_element_type=F32)
        merged = term if merged is None else merged + term
    o_ref[...] = merged.astype(BF16)


def _merge(h, ys, wg_bf, wb_bf, tm=1024, tn=512):
    t = h.shape[0]
    y_spec = pl.BlockSpec((tm, BRANCH_W), lambda i, j: (i, 0))
    return pl.pallas_call(
        _merge_kernel,
        grid=(t // tm, D_MODEL // tn),
        in_specs=[
            pl.BlockSpec((tm, D_MODEL), lambda i, j: (i, 0)),
            y_spec, y_spec, y_spec, y_spec,
            pl.BlockSpec((N_BRANCH, D_MODEL, tn), lambda i, j: (0, 0, j)),
            pl.BlockSpec((N_BRANCH, BRANCH_W, tn), lambda i, j: (0, 0, j)),
        ],
        out_specs=pl.BlockSpec((tm, tn), lambda i, j: (i, j)),
        out_shape=jax.ShapeDtypeStruct((t, D_MODEL), BF16),
        compiler_params=_params(("parallel", "arbitrary")),
        name="merge",
    )(h, *ys, wg_bf, wb_bf)


def _outproj_kernel(x_ref, m_ref, w_ref, fw_ref, o_ref, *, final_norm):
    x = x_ref[...] + jnp.dot(m_ref[...], w_ref[...], preferred_element_type=F32)
    if final_norm:
        ms = jnp.mean(x * x, axis=-1, keepdims=True)
        x = x * lax.rsqrt(ms + EPS) * fw_ref[...]
    o_ref[...] = x


def _outproj(x2, merged, w_out_bf, final_w, final_norm, tm=512):
    t = x2.shape[0]
    return pl.pallas_call(
        functools.partial(_outproj_kernel, final_norm=final_norm),
        grid=(t // tm,),
        in_specs=[
            pl.BlockSpec((tm, D_MODEL), lambda i: (i, 0)),
            pl.BlockSpec((tm, D_MODEL), lambda i: (i, 0)),
            pl.BlockSpec((D_MODEL, D_MODEL), lambda i: (0, 0)),
            pl.BlockSpec((1, D_MODEL), lambda i: (0, 0)),
        ],
        out_specs=pl.BlockSpec((tm, D_MODEL), lambda i: (i, 0)),
        out_shape=jax.ShapeDtypeStruct((t, D_MODEL), F32),
        compiler_params=_params(("parallel",)),
        name="outproj",
    )(x2, merged, w_out_bf, final_w.reshape(1, D_MODEL))


def _column_scale():
    cs = jnp.ones((IN_COLS,), F32)
    cs = cs.at[COL_NA_Q:COL_NA_Q + BRANCH_W].set(NA_HEAD_DIM ** -0.5)
    cs = cs.at[COL_DF_Q:COL_DF_Q + BRANCH_W].set(DIFF_HEAD_DIM ** -0.5 * LOG2E)
    return cs.reshape(1, IN_COLS)


def kernel(x, norm_w, w_in, na_bias, lam_q1, lam_k1, lam_q2, lam_k2, diff_subln_w, pool_w, pool_scale,
           sgu_ln_w, sgu_ln_b, sgu_ws, sgu_b, w_branch, w_gate, w_out, final_norm_w):
    batch, seq, _ = x.shape
    depth = norm_w.shape[0]
    t = batch * seq
    rows = seq // GRID_W
    x2 = x.reshape(t, D_MODEL)
    col_scale = _column_scale()
    pat_ids, row_sel, col_sel = _na_patterns(rows)
    for l in range(depth):
        h, proj, v_t = _inproj(x2, norm_w[l], w_in[l].astype(BF16), col_scale)
        y_a = _na_attention(proj, _na_dense_bias(na_bias[l], row_sel, col_sel), pat_ids, batch, seq)
        lambda_init = 0.8 - 0.6 * math.exp(-0.3 * l)
        lam_vecs = jnp.stack([lam_q1[l], lam_k1[l], lam_q2[l], lam_k2[l]]).astype(F32)
        y_b = _diff_attention(proj, v_t, lam_vecs, diff_subln_w[l], batch, seq, lambda_init)
        y_c = _pool_mixer(proj, pool_w[l].astype(BF16), pool_scale[l], batch, seq)
        bs_full = jnp.broadcast_to(sgu_b[l][:, :, None], (SGU_GROUPS, SGU_CHUNK, SGU_GROUP_W))
        y_d = _sgu_mixer(proj, sgu_ln_w[l], sgu_ln_b[l], sgu_ws[l].astype(BF16), bs_full, t)
        merged = _merge(h, (y_a, y_b, y_c, y_d), w_gate[l].astype(BF16), w_branch[l].astype(BF16))
        x2 = _outproj(x2, merged, w_out[l].astype(BF16), final_norm_w, l == depth - 1)
    return x2.reshape(batch, seq, D_MODEL)
```

```python
import functools
import math

import jax
import jax.numpy as jnp
import numpy as np
from jax import lax
from jax.experimental import pallas as pl
from jax.experimental.pallas import tpu as pltpu

D_MODEL = 2048
GRID_W = 64
N_BRANCH = 4
BRANCH_W = D_MODEL // N_BRANCH
NA_HEADS = 4
NA_HEAD_DIM = BRANCH_W // NA_HEADS
NA_ROWS_MAX = 8
NA_COLS = 16
DIFF_HEADS = 4
DIFF_HEAD_DIM = BRANCH_W // (2 * DIFF_HEADS)
POOL_WINDOWS = (2, 4, 8, 16)
POOL_GROUP_W = BRANCH_W // len(POOL_WINDOWS)
SGU_CHUNK = 128
SGU_GROUPS = 4
SGU_GROUP_W = BRANCH_W // SGU_GROUPS
IN_COLS = 13 * BRANCH_W
EPS = 1e-6

COL_NA_Q = 0
COL_NA_K = BRANCH_W
COL_NA_V = 2 * BRANCH_W
COL_DF_Q = 3 * BRANCH_W
COL_DF_K = 4 * BRANCH_W
COL_DF_V = 5 * BRANCH_W
COL_POOL = 6 * BRANCH_W
COL_SGU_U = 7 * BRANCH_W
COL_SGU_V = 8 * BRANCH_W
COL_Z = 9 * BRANCH_W

LANES = 128
VMEM_LIMIT = 56 * 1024 * 1024
MASK_VALUE = -1e30
LOG2E = math.log2(math.e)

BF16 = jnp.bfloat16
F32 = jnp.float32


def _params(sem, vmem=VMEM_LIMIT):
    return pltpu.CompilerParams(dimension_semantics=sem, vmem_limit_bytes=vmem)


def _silu(z):
    return z * (1.0 / (1.0 + jnp.exp(-z)))


def _sigmoid(z):
    return 1.0 / (1.0 + jnp.exp(-z))


def _inproj_kernel(x_ref, nw_ref, w_ref, cs_ref, h_ref, p_ref, vta_ref, vtd_ref, *, vt_places):
    @pl.when(pl.program_id(1) == 0)
    def _():
        x = x_ref[...]
        ms = jnp.mean(x * x, axis=-1, keepdims=True)
        h_ref[...] = (x * lax.rsqrt(ms + EPS) * nw_ref[...]).astype(BF16)

    acc = jnp.dot(h_ref[...], w_ref[...], preferred_element_type=F32) * cs_ref[...]
    p_ref[...] = acc.astype(BF16)

    for vt_ref, (tile, off) in zip((vta_ref, vtd_ref), vt_places):
        @pl.when(pl.program_id(1) == tile)
        def _(vt_ref=vt_ref, off=off):
            vt_ref[...] = acc[:, off:off + BRANCH_W].T.astype(BF16)


def _inproj(x2, norm_w, w_in_bf, col_scale, tm=1024, tn=IN_COLS // 4):
    t = x2.shape[0]
    vt_places = tuple(divmod(col, tn) for col in (COL_NA_V, COL_DF_V))
    for _, off in vt_places:
        assert off + BRANCH_W <= tn and off % LANES == 0, "value columns must sit inside one column tile"
    vt_spec = pl.BlockSpec((BRANCH_W, tm), lambda i, j: (0, i))
    vt_shape = jax.ShapeDtypeStruct((BRANCH_W, t), BF16)
    return pl.pallas_call(
        functools.partial(_inproj_kernel, vt_places=vt_places),
        grid=(t // tm, IN_COLS // tn),
        in_specs=[
            pl.BlockSpec((tm, D_MODEL), lambda i, j: (i, 0)),
            pl.BlockSpec((1, D_MODEL), lambda i, j: (0, 0)),
            pl.BlockSpec((D_MODEL, tn), lambda i, j: (0, j)),
            pl.BlockSpec((1, tn), lambda i, j: (0, j)),
        ],
        out_specs=[
            pl.BlockSpec((tm, D_MODEL), lambda i, j: (i, 0)),
            pl.BlockSpec((tm, tn), lambda i, j: (i, j)),
            vt_spec, vt_spec,
        ],
        out_shape=[
            jax.ShapeDtypeStruct((t, D_MODEL), BF16),
            jax.ShapeDtypeStruct((t, IN_COLS), BF16),
            vt_shape, vt_shape,
        ],
        compiler_params=_params(("parallel", "arbitrary")),
        name="inproj",
    )(x2, norm_w.reshape(1, D_MODEL), w_in_bf, col_scale)


NA_QROWS = 8
NA_KROWS = 16
NA_TQ = NA_QROWS * GRID_W
NA_TK = NA_KROWS * GRID_W
NA_KSPLIT = 4
NA_KBLK = NA_TK // NA_KSPLIT


def _na_window_start(rb, rows):
    lo = rb * NA_QROWS - NA_ROWS_MAX // 2
    return jnp.clip(lo, 0, rows - NA_KROWS)


def _na_patterns(rows):
    nrb = rows // NA_QROWS
    kr_win = min(NA_ROWS_MAX, rows)
    n_roff = 2 * NA_ROWS_MAX - 1
    n_coff = 2 * NA_COLS - 1
    pats, pat_ids = [], []
    for rb in range(nrb):
        r = rb * NA_QROWS + np.arange(NA_QROWS)
        start = int(np.clip(rb * NA_QROWS - NA_ROWS_MAX // 2, 0, rows - NA_KROWS))
        kr = start + np.arange(NA_KROWS)
        rs = np.clip(r - kr_win // 2, 0, rows - kr_win)
        ok = (kr[None, :] >= rs[:, None]) & (kr[None, :] < rs[:, None] + kr_win)
        off = kr[None, :] - r[:, None] + (NA_ROWS_MAX - 1)
        sel = np.zeros((NA_QROWS, NA_KROWS, n_roff + 1), np.float32)
        for i in range(NA_QROWS):
            for j in range(NA_KROWS):
                sel[i, j, off[i, j] if ok[i, j] else n_roff] = 1.0
        for pid, p in enumerate(pats):
            if np.array_equal(p, sel):
                pat_ids.append(pid)
                break
        else:
            pat_ids.append(len(pats))
            pats.append(sel)
    c = np.arange(GRID_W)
    cs = np.clip(c - NA_COLS // 2, 0, GRID_W - NA_COLS)
    col_sel = np.zeros((GRID_W, GRID_W, n_coff), np.float32)
    for qc in range(GRID_W):
        for kc in range(cs[qc], cs[qc] + NA_COLS):
            col_sel[qc, kc, kc - qc + (NA_COLS - 1)] = 1.0
    return pat_ids, np.stack(pats), col_sel


def _na_dense_bias(rel_bias, row_sel, col_sel):
    npat = row_sel.shape[0]
    n_sel = row_sel.shape[-1]
    assert 2 * GRID_W == LANES
    cols = jnp.einsum("hab,qkb->hakq", rel_bias.astype(F32), col_sel, precision=lax.Precision.HIGHEST) * LOG2E
    cols = jnp.where((col_sel.sum(-1) > 0).T[None, None], cols, MASK_VALUE)
    cols = jnp.concatenate([cols, jnp.full((NA_HEADS, 1, GRID_W, GRID_W), MASK_VALUE, F32)], axis=1)
    cols2 = jnp.concatenate([cols, cols], axis=-1)
    tile_ids = jnp.asarray(np.argmax(row_sel, axis=-1).reshape(-1), jnp.int32)

    def body(ids_ref, cols_ref, o_ref):
        base = pl.program_id(0) * (NA_QROWS * NA_KROWS) + pl.program_id(2)
        left = lax.broadcasted_iota(jnp.int32, (GRID_W, LANES), 1) < GRID_W
        for ii in range(NA_QROWS // 2):
            even = cols_ref[ids_ref[base + 2 * ii * NA_KROWS]]
            odd = cols_ref[ids_ref[base + (2 * ii + 1) * NA_KROWS]]
            o_ref[:, ii * LANES:(ii + 1) * LANES] = jnp.where(left, even, odd)

    return pl.pallas_call(
        body,
        grid_spec=pltpu.PrefetchScalarGridSpec(
            num_scalar_prefetch=1,
            grid=(npat, NA_HEADS, NA_KROWS),
            in_specs=[pl.BlockSpec((None, n_sel, GRID_W, LANES), lambda p, h, j, ids: (h, 0, 0, 0))],
            out_specs=pl.BlockSpec((None, None, GRID_W, NA_TQ), lambda p, h, j, ids: (p, h, j, 0)),
        ),
        out_shape=jax.ShapeDtypeStruct((npat, NA_HEADS, NA_TK, NA_TQ), F32),
        compiler_params=_params(("parallel", "parallel", "parallel")),
        name="na_bias_expand",
    )(tile_ids, cols2)


def _na_kernel(q_ref, k0, k1, k2, k3, v0, v1, v2, v3, z_ref, b_ref, o_ref):
    nt = (((1,), (1,)), ((), ()))
    k_all = jnp.concatenate([r[...] for r in (k0, k1, k2, k3)], axis=0)
    vt_all = jnp.concatenate([r[...] for r in (v0, v1, v2, v3)], axis=1)

    def head(h):
        return slice(h * NA_HEAD_DIM, (h + 1) * NA_HEAD_DIM)

    def scores(h):
        st = lax.dot_general(k_all[:, head(h)], q_ref[:, head(h)], nt, preferred_element_type=F32)
        return st + b_ref[h]

    def softmax(st):
        p = jnp.exp2(st - jnp.max(st, axis=0, keepdims=True))
        return jnp.sum(p, axis=0, keepdims=True), p.astype(BF16)

    def output(l, p, h):
        ot = jnp.dot(vt_all[head(h)], p, preferred_element_type=F32) / l
        o_ref[:, head(h)] = (ot.T * _silu(z_ref[:, head(h)].astype(F32))).astype(BF16)

    st_next = scores(0)
    pending = None
    for h in range(NA_HEADS):
        st = st_next
        if h + 1 < NA_HEADS:
            st_next = scores(h + 1)
        lp = softmax(st)
        if pending is not None:
            output(*pending)
        pending = (*lp, h)
    output(*pending)


def _na_attention(proj, v_t, dense_bias, pat_ids, batch, seq):
    rows = seq // GRID_W
    nrb = rows // NA_QROWS
    t = batch * seq
    qblk_per_seq = seq // NA_TQ
    kblk_per_seq = seq // NA_KBLK
    kblk_per_row = NA_KBLK // GRID_W

    def q_map(col):
        return lambda rb, b: (b * qblk_per_seq + rb, col)

    def k_map(j):
        def f(rb, b):
            start = _na_window_start(rb, rows) // kblk_per_row
            return (b * kblk_per_seq + start + j, COL_NA_K // BRANCH_W)
        return f

    def vt_map(j):
        return lambda rb, b: (0, k_map(j)(rb, b)[0])

    def bias_map(rb, b):
        pid = 0
        for r, p in enumerate(pat_ids):
            pid = pid + jnp.where(rb == r, p, 0)
        return (pid, 0, 0, 0)

    in_specs = [pl.BlockSpec((NA_TQ, BRANCH_W), q_map(COL_NA_Q // BRANCH_W))]
    in_specs += [pl.BlockSpec((NA_KBLK, BRANCH_W), k_map(j)) for j in range(NA_KSPLIT)]
    in_specs += [pl.BlockSpec((BRANCH_W, NA_KBLK), vt_map(j)) for j in range(NA_KSPLIT)]
    in_specs += [pl.BlockSpec((NA_TQ, BRANCH_W), q_map(COL_Z // BRANCH_W + 0))]
    in_specs += [pl.BlockSpec((None, NA_HEADS, NA_TK, NA_TQ), bias_map)]
    return pl.pallas_call(
        _na_kernel,
        grid=(nrb, batch),
        in_specs=in_specs,
        out_specs=pl.BlockSpec((NA_TQ, BRANCH_W), lambda rb, b: (b * qblk_per_seq + rb, 0)),
        out_shape=jax.ShapeDtypeStruct((t, BRANCH_W), BF16),
        compiler_params=_params(("parallel", "parallel")),
        name="na_attention",
    )(*([proj] * 5), *([v_t] * 4), proj, dense_bias)


DIFF_QCHUNK = 512
DIFF_KCHUNK = 1024


def _alibi_slopes():
    return [2.0 ** (-8.0 * (h + 1) / DIFF_HEADS) * LOG2E for h in range(DIFF_HEADS)]


def _split3(x):
    def trunc(v):
        return (v.view(np.uint32) & np.uint32(0xFFFF0000)).view(np.float32)
    x1 = trunc(x)
    r1 = x - x1
    x2 = trunc(r1)
    return x1, x2, r1 - x2


def _alibi_features(seq):
    pos = np.arange(seq, dtype=np.float32)
    slopes = np.asarray(_alibi_slopes(), np.float32)[:, None]
    u = _split3(-slopes * pos[None, :])
    w = _split3(slopes * pos[None, :])
    one = np.ones((DIFF_HEADS, seq), np.float32)
    zero = np.zeros((DIFF_HEADS, seq, LANES - 6), np.float32)
    qfeat = np.concatenate([np.stack(list(u) + [one] * 3, axis=-1), zero], axis=-1)
    kfeat = np.concatenate([np.stack([one] * 3 + list(w), axis=-1), zero], axis=-1)
    return jnp.asarray(qfeat).astype(BF16), jnp.asarray(kfeat).astype(BF16)


def _diff_kernel(lam_ref, sw_ref, q_ref, qft_ref, k_ref, kf_ref, vt_ref, corr_ref, z_ref, o_ref,
                 qa_ref, m_ref, l_ref, acc_ref, *, tq, tk, lambda_init):
    qi = pl.program_id(2)
    kj = pl.program_id(3)
    hw = 2 * DIFF_HEAD_DIM

    @pl.when(kj == 0)
    def _():
        m_ref[...] = jnp.full(m_ref.shape, MASK_VALUE, F32)
        l_ref[...] = jnp.zeros(l_ref.shape, F32)
        acc_ref[...] = jnp.zeros(acc_ref.shape, F32)
        qt = q_ref[...].astype(F32).T
        row = lax.broadcasted_iota(jnp.int32, qt.shape, 0)
        for c in range(2):
            in_c = (row >= c * DIFF_HEAD_DIM) & (row < (c + 1) * DIFF_HEAD_DIM)
            qa_ref[c, :hw, :] = jnp.where(in_c, qt, 0.0).astype(BF16)
            qa_ref[c, hw:, :] = qft_ref[...]

    keys_after = kj > qi
    k = k_ref[...]
    kf = kf_ref[...]
    vt = vt_ref[...]

    def step(diag):
        if diag:
            ka_pos = jnp.concatenate([k, kf], axis=1)
            ka_neg = jnp.concatenate([k, -kf], axis=1)
        else:
            ka_any = jnp.concatenate([k, jnp.where(keys_after, -kf, kf)], axis=1)
        chunks = [(kk, c, n) for kk in range(tk // DIFF_KCHUNK) for c in range(2)
                  for n in range(tq // DIFF_QCHUNK)]

        def slices(kk, c, n):
            return (slice(kk * DIFF_KCHUNK, (kk + 1) * DIFF_KCHUNK),
                    slice(n * DIFF_QCHUNK, (n + 1) * DIFF_QCHUNK))

        def scores(kk, c, n):
            ks, cs = slices(kk, c, n)
            qa = qa_ref[c, :, cs]
            if not diag:
                return jnp.dot(ka_any[ks], qa, preferred_element_type=F32)
            k_lo, q_lo = kk * DIFF_KCHUNK, n * DIFF_QCHUNK
            if k_lo + DIFF_KCHUNK - 1 <= q_lo:
                return jnp.dot(ka_pos[ks], qa, preferred_element_type=F32)
            if k_lo >= q_lo + DIFF_QCHUNK - 1:
                return jnp.dot(ka_neg[ks], qa, preferred_element_type=F32)
            off = q_lo - k_lo
            st = jnp.dot(ka_pos[ks], qa, preferred_element_type=F32)
            return st + corr_ref[:, off:off + DIFF_QCHUNK]

        def softmax(st, kk, c, n):
            _, cs = slices(kk, c, n)
            m_prev = m_ref[c, :, cs]
            m_new = jnp.maximum(m_prev, jnp.max(st, axis=0, keepdims=True))
            alpha = jnp.exp2(m_prev - m_new)
            p = jnp.exp2(st - m_new)
            l_ref[c, :, cs] = alpha * l_ref[c, :, cs] + jnp.sum(p, axis=0, keepdims=True)
            m_ref[c, :, cs] = m_new
            return alpha, p.astype(BF16)

        def accumulate(alpha, p, kk, c, n):
            ks, cs = slices(kk, c, n)
            acc_ref[c, :, cs] = alpha * acc_ref[c, :, cs] + jnp.dot(vt[:, ks], p, preferred_element_type=F32)

        nc = len(chunks)
        st_q = {0: scores(*chunks[0])}
        if nc > 1:
            st_q[1] = scores(*chunks[1])
        p_q = {0: softmax(st_q.pop(0), *chunks[0])}
        for i in range(nc):
            if i + 2 < nc:
                st_q[i + 2] = scores(*chunks[i + 2])
            if i + 1 < nc:
                p_q[i + 1] = softmax(st_q.pop(i + 1), *chunks[i + 1])
            accumulate(*p_q.pop(i), *chunks[i])

    pl.when(qi != kj)(lambda: step(False))
    pl.when(qi == kj)(lambda: step(True))

    @pl.when(kj == pl.num_programs(3) - 1)
    def _():
        lv = lam_ref[...]
        lam = (jnp.exp(jnp.sum(lv[0:1] * lv[1:2], axis=-1, keepdims=True))
               - jnp.exp(jnp.sum(lv[2:3] * lv[3:4], axis=-1, keepdims=True)) + lambda_init)
        at = acc_ref[0] / l_ref[0] - lam * (acc_ref[1] / l_ref[1])
        ms = jnp.mean(at * at, axis=0, keepdims=True)
        y = (at * lax.rsqrt(ms + EPS)).T * sw_ref[...] * (1.0 - lambda_init)
        o_ref[...] = (y * _silu(z_ref[...].astype(F32))).astype(BF16)


def _diff_attention(proj, v_t, lam_vecs, subln_w, batch, seq, lambda_init, tq=2048, tk=2048):
    t = batch * seq
    nq = seq // tq
    nk = seq // tk
    hw = 2 * DIFF_HEAD_DIM
    assert tq == tk and DIFF_KCHUNK % DIFF_QCHUNK == 0, "diagonal handling assumes aligned square tiles"
    qfeat, kfeat = _alibi_features(seq)
    rel = jnp.arange(DIFF_KCHUNK, dtype=F32)[:, None] - jnp.arange(DIFF_KCHUNK, dtype=F32)[None, :]
    corr = -2.0 * jnp.asarray(_alibi_slopes(), F32)[:, None, None] * jnp.maximum(rel, 0.0)[None]
    kern = functools.partial(_diff_kernel, tq=tq, tk=tk, lambda_init=lambda_init)
    return pl.pallas_call(
        kern,
        grid=(batch, DIFF_HEADS, nq, nk),
        in_specs=[
            pl.BlockSpec((4, DIFF_HEAD_DIM), lambda b, h, i, j: (0, 0)),
            pl.BlockSpec((1, hw), lambda b, h, i, j: (0, 0)),
            pl.BlockSpec((tq, hw), lambda b, h, i, j: (b * nq + i, COL_DF_Q // hw + h)),
            pl.BlockSpec((None, LANES, tq), lambda b, h, i, j: (h, 0, i)),
            pl.BlockSpec((tk, hw), lambda b, h, i, j: (b * nk + j, COL_DF_K // hw + h)),
            pl.BlockSpec((None, tk, LANES), lambda b, h, i, j: (h, j, 0)),
            pl.BlockSpec((hw, tk), lambda b, h, i, j: (h, b * nk + j)),
            pl.BlockSpec((None, DIFF_KCHUNK, DIFF_KCHUNK), lambda b, h, i, j: (h, 0, 0)),
            pl.BlockSpec((tq, hw), lambda b, h, i, j: (b * nq + i, (COL_Z + BRANCH_W) // hw + h)),
        ],
        out_specs=pl.BlockSpec((tq, hw), lambda b, h, i, j: (b * nq + i, h)),
        out_shape=jax.ShapeDtypeStruct((t, BRANCH_W), BF16),
        scratch_shapes=[
            pltpu.VMEM((2, 2 * hw, tq), BF16),
            pltpu.VMEM((2, 1, tq), F32),
            pltpu.VMEM((2, 1, tq), F32),
            pltpu.VMEM((2, hw, tq), F32),
        ],
        compiler_params=_params(("parallel", "parallel", "parallel", "arbitrary")),
        name="diff_attention",
    )(lam_vecs, subln_w.reshape(1, hw), proj, qfeat.transpose(0, 2, 1), proj, kfeat, v_t, corr, proj)


POOL_PAD = 8


def _pool_kernel(x_ref, z_ref, w_ref, sc_ref, o_ref, pad_ref, *, seq):
    t = lax.broadcasted_iota(jnp.int32, (seq, POOL_GROUP_W), 0)
    zeros = jnp.zeros((POOL_PAD, POOL_GROUP_W), F32)
    pad_ref[0:POOL_PAD, :] = zeros
    pad_ref[POOL_PAD + seq:, :] = zeros
    for g, win in enumerate(POOL_WINDOWS):
        gs = slice(g * POOL_GROUP_W, (g + 1) * POOL_GROUP_W)
        x = x_ref[:, gs].astype(F32)
        pad_ref[POOL_PAD:POOL_PAD + seq, :] = x
        half = win // 2
        total = x
        for d in range(-half, win - half):
            if d != 0:
                total = total + pad_ref[POOL_PAD + d:POOL_PAD + d + seq, :]
        cnt = jnp.minimum(t + (win - half), seq) - jnp.maximum(t - half, 0)
        pooled = total / cnt.astype(F32) - x
        y = jnp.dot(pooled.astype(BF16), w_ref[g], preferred_element_type=F32) * sc_ref[:, gs]
        o_ref[:, gs] = (y * _silu(z_ref[:, gs].astype(F32))).astype(BF16)


def _pool_mixer(proj, pool_w_bf, pool_scale, batch, seq):
    t = batch * seq
    assert POOL_PAD >= max(POOL_WINDOWS) // 2
    return pl.pallas_call(
        functools.partial(_pool_kernel, seq=seq),
        grid=(batch,),
        in_specs=[
            pl.BlockSpec((seq, BRANCH_W), lambda b: (b, COL_POOL // BRANCH_W)),
            pl.BlockSpec((seq, BRANCH_W), lambda b: (b, COL_Z // BRANCH_W + 2)),
            pl.BlockSpec((len(POOL_WINDOWS), POOL_GROUP_W, POOL_GROUP_W), lambda b: (0, 0, 0)),
            pl.BlockSpec((1, BRANCH_W), lambda b: (0, 0)),
        ],
        out_specs=pl.BlockSpec((seq, BRANCH_W), lambda b: (b, 0)),
        out_shape=jax.ShapeDtypeStruct((t, BRANCH_W), BF16),
        scratch_shapes=[pltpu.VMEM((seq + 2 * POOL_PAD, POOL_GROUP_W), F32)],
        compiler_params=_params(("parallel",)),
        name="pool_mixer",
    )(proj, proj, pool_w_bf, pool_scale.reshape(1, BRANCH_W))


def _sgu_kernel(u_ref, v_ref, z_ref, lw_ref, lb_ref, ws_ref, bs_ref, o_ref, *, tm):
    v = v_ref[...].astype(F32)
    mu = jnp.mean(v, axis=-1, keepdims=True)
    vc = v - mu
    var = jnp.mean(vc * vc, axis=-1, keepdims=True)
    vn = (vc * lax.rsqrt(var + EPS) * lw_ref[...] + lb_ref[...]).astype(BF16)
    for n in range(tm // SGU_CHUNK):
        rs = slice(n * SGU_CHUNK, (n + 1) * SGU_CHUNK)
        for g in range(SGU_GROUPS):
            gs = slice(g * SGU_GROUP_W, (g + 1) * SGU_GROUP_W)
            mixed = jnp.dot(ws_ref[g], vn[rs, gs], preferred_element_type=F32) + bs_ref[g]
            y = u_ref[rs, gs].astype(F32) * mixed
            o_ref[rs, gs] = (y * _silu(z_ref[rs, gs].astype(F32))).astype(BF16)


def _sgu_mixer(proj, ln_w, ln_b, ws_bf, bs_full, t, tm=2048):
    return pl.pallas_call(
        functools.partial(_sgu_kernel, tm=tm),
        grid=(t // tm,),
        in_specs=[
            pl.BlockSpec((tm, BRANCH_W), lambda i: (i, COL_SGU_U // BRANCH_W)),
            pl.BlockSpec((tm, BRANCH_W), lambda i: (i, COL_SGU_V // BRANCH_W)),
            pl.BlockSpec((tm, BRANCH_W), lambda i: (i, COL_Z // BRANCH_W + 3)),
            pl.BlockSpec((1, BRANCH_W), lambda i: (0, 0)),
            pl.BlockSpec((1, BRANCH_W), lambda i: (0, 0)),
            pl.BlockSpec((SGU_GROUPS, SGU_CHUNK, SGU_CHUNK), lambda i: (0, 0, 0)),
            pl.BlockSpec((SGU_GROUPS, SGU_CHUNK, SGU_GROUP_W), lambda i: (0, 0, 0)),
        ],
        out_specs=pl.BlockSpec((tm, BRANCH_W), lambda i: (i, 0)),
        out_shape=jax.ShapeDtypeStruct((t, BRANCH_W), BF16),
        compiler_params=_params(("parallel",)),
        name="sgu_mixer",
    )(proj, proj, proj, ln_w.reshape(1, BRANCH_W), ln_b.reshape(1, BRANCH_W), ws_bf, bs_full)


def _merge_kernel(h_ref, ya_ref, yb_ref, yc_ref, yd_ref, wg_ref, wb_ref, o_ref):
    h = h_ref[...]
    merged = None
    for i, y_ref in enumerate((ya_ref, yb_ref, yc_ref, yd_ref)):
        gate = _sigmoid(jnp.dot(h, wg_ref[i], preferred_element_type=F32))
        term = gate * jnp.dot(y_ref[...], wb_ref[i], preferred_element_type=F32)
        merged = term if merged is None else merged + term
    o_ref[...] = merged.astype(BF16)


def _merge(h, ys, wg_bf, wb_bf, tm=1024, tn=512):
    t = h.shape[0]
    y_spec = pl.BlockSpec((tm, BRANCH_W), lambda i, j: (i, 0))
    return pl.pallas_call(
        _merge_kernel,
        grid=(t // tm, D_MODEL // tn),
        in_specs=[
            pl.BlockSpec((tm, D_MODEL), lambda i, j: (i, 0)),
            y_spec, y_spec, y_spec, y_spec,
            pl.BlockSpec((N_BRANCH, D_MODEL, tn), lambda i, j: (0, 0, j)),
            pl.BlockSpec((N_BRANCH, BRANCH_W, tn), lambda i, j: (0, 0, j)),
        ],
        out_specs=pl.BlockSpec((tm, tn), lambda i, j: (i, j)),
        out_shape=jax.ShapeDtypeStruct((t, D_MODEL), BF16),
        compiler_params=_params(("parallel", "arbitrary")),
        name="merge",
    )(h, *ys, wg_bf, wb_bf)


def _outproj_kernel(x_ref, m_ref, w_ref, fw_ref, o_ref, *, final_norm):
    x = x_ref[...] + jnp.dot(m_ref[...], w_ref[...], preferred_element_type=F32)
    if final_norm:
        ms = jnp.mean(x * x, axis=-1, keepdims=True)
        x = x * lax.rsqrt(ms + EPS) * fw_ref[...]
    o_ref[...] = x


def _outproj(x2, merged, w_out_bf, final_w, final_norm, tm=512):
    t = x2.shape[0]
    return pl.pallas_call(
        functools.partial(_outproj_kernel, final_norm=final_norm),
        grid=(t // tm,),
        in_specs=[
            pl.BlockSpec((tm, D_MODEL), lambda i: (i, 0)),
            pl.BlockSpec((tm, D_MODEL), lambda i: (i, 0)),
            pl.BlockSpec((D_MODEL, D_MODEL), lambda i: (0, 0)),
            pl.BlockSpec((1, D_MODEL), lambda i: (0, 0)),
        ],
        out_specs=pl.BlockSpec((tm, D_MODEL), lambda i: (i, 0)),
        out_shape=jax.ShapeDtypeStruct((t, D_MODEL), F32),
        compiler_params=_params(("parallel",)),
        name="outproj",
    )(x2, merged, w_out_bf, final_w.reshape(1, D_MODEL))


def _column_scale():
    cs = jnp.ones((IN_COLS,), F32)
    cs = cs.at[COL_NA_Q:COL_NA_Q + BRANCH_W].set(NA_HEAD_DIM ** -0.5 * LOG2E)
    cs = cs.at[COL_DF_Q:COL_DF_Q + BRANCH_W].set(DIFF_HEAD_DIM ** -0.5 * LOG2E)
    return cs.reshape(1, IN_COLS)


def kernel(x, norm_w, w_in, na_bias, lam_q1, lam_k1, lam_q2, lam_k2, diff_subln_w, pool_w, pool_scale,
           sgu_ln_w, sgu_ln_b, sgu_ws, sgu_b, w_branch, w_gate, w_out, final_norm_w):
    batch, seq, _ = x.shape
    depth = norm_w.shape[0]
    t = batch * seq
    rows = seq // GRID_W
    x2 = x.reshape(t, D_MODEL)
    col_scale = _column_scale()
    pat_ids, row_sel, col_sel = _na_patterns(rows)
    for l in range(depth):
        h, proj, vt_na, vt_df = _inproj(x2, norm_w[l], w_in[l].astype(BF16), col_scale)
        y_a = _na_attention(proj, vt_na, _na_dense_bias(na_bias[l], row_sel, col_sel), pat_ids, batch, seq)
        lambda_init = 0.8 - 0.6 * math.exp(-0.3 * l)
        lam_vecs = jnp.stack([lam_q1[l], lam_k1[l], lam_q2[l], lam_k2[l]]).astype(F32)
        y_b = _diff_attention(proj, vt_df, lam_vecs, diff_subln_w[l], batch, seq, lambda_init)
        y_c = _pool_mixer(proj, pool_w[l].astype(BF16), pool_scale[l], batch, seq)
        bs_full = jnp.broadcast_to(sgu_b[l][:, :, None], (SGU_GROUPS, SGU_CHUNK, SGU_GROUP_W))
        y_d = _sgu_mixer(proj, sgu_ln_w[l], sgu_ln_b[l], sgu_ws[l].astype(BF16), bs_full, t)
        merged = _merge(h, (y_a, y_b, y_c, y_d), w_gate[l].astype(BF16), w_branch[l].astype(BF16))
        x2 = _outproj(x2, merged, w_out[l].astype(BF16), final_norm_w, l == depth - 1)
    return x2.reshape(batch, seq, D_MODEL)
```

```python
import functools
import math

import jax
import jax.numpy as jnp
import numpy as np
from jax import lax
from jax.experimental import pallas as pl
from jax.experimental.pallas import tpu as pltpu

D_MODEL = 2048
GRID_W = 64
N_BRANCH = 4
BRANCH_W = D_MODEL // N_BRANCH
NA_HEADS = 4
NA_HEAD_DIM = BRANCH_W // NA_HEADS
NA_ROWS_MAX = 8
NA_COLS = 16
DIFF_HEADS = 4
DIFF_HEAD_DIM = BRANCH_W // (2 * DIFF_HEADS)
POOL_WINDOWS = (2, 4, 8, 16)
POOL_GROUP_W = BRANCH_W // len(POOL_WINDOWS)
SGU_CHUNK = 128
SGU_GROUPS = 4
SGU_GROUP_W = BRANCH_W // SGU_GROUPS
IN_COLS = 13 * BRANCH_W
EPS = 1e-6

COL_NA_Q = 0
COL_NA_K = BRANCH_W
COL_NA_V = 2 * BRANCH_W
COL_DF_Q = 3 * BRANCH_W
COL_DF_K = 4 * BRANCH_W
COL_DF_V = 5 * BRANCH_W
COL_POOL = 6 * BRANCH_W
COL_SGU_U = 7 * BRANCH_W
COL_SGU_V = 8 * BRANCH_W
COL_Z = 9 * BRANCH_W

LANES = 128
VMEM_LIMIT = 56 * 1024 * 1024
MASK_VALUE = -1e30
LOG2E = math.log2(math.e)

BF16 = jnp.bfloat16
F32 = jnp.float32


def _params(sem, vmem=VMEM_LIMIT):
    return pltpu.CompilerParams(dimension_semantics=sem, vmem_limit_bytes=vmem)


def _silu(z):
    return z * (1.0 / (1.0 + jnp.exp(-z)))


def _sigmoid(z):
    return 1.0 / (1.0 + jnp.exp(-z))


def _inproj_kernel(x_ref, nw_ref, w_ref, cs_ref, h_ref, p_ref, vta_ref, vtd_ref, *, vt_places):
    @pl.when(pl.program_id(1) == 0)
    def _():
        x = x_ref[...]
        ms = jnp.mean(x * x, axis=-1, keepdims=True)
        h_ref[...] = (x * lax.rsqrt(ms + EPS) * nw_ref[...]).astype(BF16)

    acc = jnp.dot(h_ref[...], w_ref[...], preferred_element_type=F32) * cs_ref[...]
    p_ref[...] = acc.astype(BF16)

    for vt_ref, (tile, off) in zip((vta_ref, vtd_ref), vt_places):
        @pl.when(pl.program_id(1) == tile)
        def _(vt_ref=vt_ref, off=off):
            vt_ref[...] = acc[:, off:off + BRANCH_W].T.astype(BF16)


def _inproj(x2, norm_w, w_in_bf, col_scale, tm=1024, tn=IN_COLS // 4):
    t = x2.shape[0]
    vt_places = tuple(divmod(col, tn) for col in (COL_NA_V, COL_DF_V))
    for _, off in vt_places:
        assert off + BRANCH_W <= tn and off % LANES == 0, "value columns must sit inside one column tile"
    vt_spec = pl.BlockSpec((BRANCH_W, tm), lambda i, j: (0, i))
    vt_shape = jax.ShapeDtypeStruct((BRANCH_W, t), BF16)
    return pl.pallas_call(
        functools.partial(_inproj_kernel, vt_places=vt_places),
        grid=(t // tm, IN_COLS // tn),
        in_specs=[
            pl.BlockSpec((tm, D_MODEL), lambda i, j: (i, 0)),
            pl.BlockSpec((1, D_MODEL), lambda i, j: (0, 0)),
            pl.BlockSpec((D_MODEL, tn), lambda i, j: (0, j)),
            pl.BlockSpec((1, tn), lambda i, j: (0, j)),
        ],
        out_specs=[
            pl.BlockSpec((tm, D_MODEL), lambda i, j: (i, 0)),
            pl.BlockSpec((tm, tn), lambda i, j: (i, j)),
            vt_spec, vt_spec,
        ],
        out_shape=[
            jax.ShapeDtypeStruct((t, D_MODEL), BF16),
            jax.ShapeDtypeStruct((t, IN_COLS), BF16),
            vt_shape, vt_shape,
        ],
        compiler_params=_params(("parallel", "arbitrary")),
        name="inproj",
    )(x2, norm_w.reshape(1, D_MODEL), w_in_bf, col_scale)


NA_QROWS = 8
NA_KROWS = 16
NA_TQ = NA_QROWS * GRID_W
NA_TK = NA_KROWS * GRID_W
NA_KSPLIT = 4
NA_KBLK = NA_TK // NA_KSPLIT


def _na_window_start(rb, rows):
    lo = rb * NA_QROWS - NA_ROWS_MAX // 2
    return jnp.clip(lo, 0, rows - NA_KROWS)


def _na_patterns(rows):
    nrb = rows // NA_QROWS
    kr_win = min(NA_ROWS_MAX, rows)
    n_roff = 2 * NA_ROWS_MAX - 1
    n_coff = 2 * NA_COLS - 1
    pats, pat_ids = [], []
    for rb in range(nrb):
        r = rb * NA_QROWS + np.arange(NA_QROWS)
        start = int(np.clip(rb * NA_QROWS - NA_ROWS_MAX // 2, 0, rows - NA_KROWS))
        kr = start + np.arange(NA_KROWS)
        rs = np.clip(r - kr_win // 2, 0, rows - kr_win)
        ok = (kr[None, :] >= rs[:, None]) & (kr[None, :] < rs[:, None] + kr_win)
        off = kr[None, :] - r[:, None] + (NA_ROWS_MAX - 1)
        sel = np.zeros((NA_QROWS, NA_KROWS, n_roff + 1), np.float32)
        for i in range(NA_QROWS):
            for j in range(NA_KROWS):
                sel[i, j, off[i, j] if ok[i, j] else n_roff] = 1.0
        for pid, p in enumerate(pats):
            if np.array_equal(p, sel):
                pat_ids.append(pid)
                break
        else:
            pat_ids.append(len(pats))
            pats.append(sel)
    c = np.arange(GRID_W)
    cs = np.clip(c - NA_COLS // 2, 0, GRID_W - NA_COLS)
    col_sel = np.zeros((GRID_W, GRID_W, n_coff), np.float32)
    for qc in range(GRID_W):
        for kc in range(cs[qc], cs[qc] + NA_COLS):
            col_sel[qc, kc, kc - qc + (NA_COLS - 1)] = 1.0
    return pat_ids, np.stack(pats), col_sel


def _na_dense_bias(rel_bias, row_sel, col_sel):
    npat = row_sel.shape[0]
    n_sel = row_sel.shape[-1]
    assert 2 * GRID_W == LANES
    cols = jnp.einsum("hab,qkb->hakq", rel_bias.astype(F32), col_sel, precision=lax.Precision.HIGHEST) * LOG2E
    cols = jnp.where((col_sel.sum(-1) > 0).T[None, None], cols, MASK_VALUE)
    cols = jnp.concatenate([cols, jnp.full((NA_HEADS, 1, GRID_W, GRID_W), MASK_VALUE, F32)], axis=1)
    cols2 = jnp.concatenate([cols, cols], axis=-1)
    tile_ids = jnp.asarray(np.argmax(row_sel, axis=-1).reshape(-1), jnp.int32)

    def body(ids_ref, cols_ref, o_ref):
        base = pl.program_id(0) * (NA_QROWS * NA_KROWS)
        left = lax.broadcasted_iota(jnp.int32, (GRID_W, LANES), 1) < GRID_W
        for j in range(NA_KROWS):
            for ii in range(NA_QROWS // 2):
                even = cols_ref[ids_ref[base + 2 * ii * NA_KROWS + j]]
                odd = cols_ref[ids_ref[base + (2 * ii + 1) * NA_KROWS + j]]
                o_ref[j * GRID_W:(j + 1) * GRID_W, ii * LANES:(ii + 1) * LANES] = jnp.where(left, even, odd)

    return pl.pallas_call(
        body,
        grid_spec=pltpu.PrefetchScalarGridSpec(
            num_scalar_prefetch=1,
            grid=(npat, NA_HEADS),
            in_specs=[pl.BlockSpec((None, n_sel, GRID_W, LANES), lambda p, h, ids: (h, 0, 0, 0))],
            out_specs=pl.BlockSpec((None, None, NA_TK, NA_TQ), lambda p, h, ids: (p, h, 0, 0)),
        ),
        out_shape=jax.ShapeDtypeStruct((npat, NA_HEADS, NA_TK, NA_TQ), F32),
        compiler_params=_params(("parallel", "parallel")),
        name="na_bias_expand",
    )(tile_ids, cols2)


def _na_kernel(q_ref, k0, k1, k2, k3, v0, v1, v2, v3, z_ref, b_ref, o_ref):
    nt = (((1,), (1,)), ((), ()))
    k_all = jnp.concatenate([r[...] for r in (k0, k1, k2, k3)], axis=0)
    vt_all = jnp.concatenate([r[...] for r in (v0, v1, v2, v3)], axis=1)

    def head(h):
        return slice(h * NA_HEAD_DIM, (h + 1) * NA_HEAD_DIM)

    def scores(h):
        st = lax.dot_general(k_all[:, head(h)], q_ref[:, head(h)], nt, preferred_element_type=F32)
        return st + b_ref[h]

    def softmax(st):
        p = jnp.exp2(st - jnp.max(st, axis=0, keepdims=True))
        return jnp.sum(p, axis=0, keepdims=True), p.astype(BF16)

    def output(l, p, h):
        ot = jnp.dot(vt_all[head(h)], p, preferred_element_type=F32) / l
        o_ref[:, head(h)] = (ot.T * _silu(z_ref[:, head(h)].astype(F32))).astype(BF16)

    st_next = scores(0)
    pending = None
    for h in range(NA_HEADS):
        st = st_next
        if h + 1 < NA_HEADS:
            st_next = scores(h + 1)
        lp = softmax(st)
        if pending is not None:
            output(*pending)
        pending = (*lp, h)
    output(*pending)


def _na_attention(proj, v_t, dense_bias, pat_ids, batch, seq):
    rows = seq // GRID_W
    nrb = rows // NA_QROWS
    t = batch * seq
    qblk_per_seq = seq // NA_TQ
    kblk_per_seq = seq // NA_KBLK
    kblk_per_row = NA_KBLK // GRID_W

    def q_map(col):
        return lambda rb, b: (b * qblk_per_seq + rb, col)

    def k_map(j):
        def f(rb, b):
            start = _na_window_start(rb, rows) // kblk_per_row
            return (b * kblk_per_seq + start + j, COL_NA_K // BRANCH_W)
        return f

    def vt_map(j):
        return lambda rb, b: (0, k_map(j)(rb, b)[0])

    def bias_map(rb, b):
        pid = 0
        for r, p in enumerate(pat_ids):
            pid = pid + jnp.where(rb == r, p, 0)
        return (pid, 0, 0, 0)

    in_specs = [pl.BlockSpec((NA_TQ, BRANCH_W), q_map(COL_NA_Q // BRANCH_W))]
    in_specs += [pl.BlockSpec((NA_KBLK, BRANCH_W), k_map(j)) for j in range(NA_KSPLIT)]
    in_specs += [pl.BlockSpec((BRANCH_W, NA_KBLK), vt_map(j)) for j in range(NA_KSPLIT)]
    in_specs += [pl.BlockSpec((NA_TQ, BRANCH_W), q_map(COL_Z // BRANCH_W + 0))]
    in_specs += [pl.BlockSpec((None, NA_HEADS, NA_TK, NA_TQ), bias_map)]
    return pl.pallas_call(
        _na_kernel,
        grid=(nrb, batch),
        in_specs=in_specs,
        out_specs=pl.BlockSpec((NA_TQ, BRANCH_W), lambda rb, b: (b * qblk_per_seq + rb, 0)),
        out_shape=jax.ShapeDtypeStruct((t, BRANCH_W), BF16),
        compiler_params=_params(("parallel", "parallel")),
        name="na_attention",
    )(*([proj] * 5), *([v_t] * 4), proj, dense_bias)


DIFF_QCHUNK = 512
DIFF_KCHUNK = 1024


def _alibi_slopes():
    return [2.0 ** (-8.0 * (h + 1) / DIFF_HEADS) * LOG2E for h in range(DIFF_HEADS)]


def _split3(x):
    def trunc(v):
        return (v.view(np.uint32) & np.uint32(0xFFFF0000)).view(np.float32)
    x1 = trunc(x)
    r1 = x - x1
    x2 = trunc(r1)
    return x1, x2, r1 - x2


def _alibi_features(seq):
    pos = np.arange(seq, dtype=np.float32)
    slopes = np.asarray(_alibi_slopes(), np.float32)[:, None]
    u = _split3(-slopes * pos[None, :])
    w = _split3(slopes * pos[None, :])
    one = np.ones((DIFF_HEADS, seq), np.float32)
    zero = np.zeros((DIFF_HEADS, seq, LANES - 6), np.float32)
    qfeat = np.concatenate([np.stack(list(u) + [one] * 3, axis=-1), zero], axis=-1)
    kfeat = np.concatenate([np.stack([one] * 3 + list(w), axis=-1), zero], axis=-1)
    return jnp.asarray(qfeat).astype(BF16), jnp.asarray(kfeat).astype(BF16)


def _diff_kernel(lam_ref, sw_ref, q_ref, qf_ref, k_ref, kf_ref, vt_ref, corr_ref, z_ref, o_ref,
                 qa_ref, m_ref, l_ref, acc_ref, *, tq, tk, lambda_init):
    qi = pl.program_id(2)
    kj = pl.program_id(3)
    nt = (((1,), (1,)), ((), ()))
    hw = 2 * DIFF_HEAD_DIM

    @pl.when(kj == 0)
    def _():
        m_ref[...] = jnp.full(m_ref.shape, MASK_VALUE, F32)
        l_ref[...] = jnp.zeros(l_ref.shape, F32)
        acc_ref[...] = jnp.zeros(acc_ref.shape, F32)
        q = q_ref[...]
        lane = lax.broadcasted_iota(jnp.int32, q.shape, 1)
        for c in range(2):
            in_c = (lane >= c * DIFF_HEAD_DIM) & (lane < (c + 1) * DIFF_HEAD_DIM)
            qa_ref[c, :, :hw] = jnp.where(in_c, q, jnp.zeros_like(q))
            qa_ref[c, :, hw:] = qf_ref[...]

    keys_after = kj > qi
    k = k_ref[...]
    kf = kf_ref[...]
    vt = vt_ref[...]

    def step(diag):
        if diag:
            ka_pos = jnp.concatenate([k, kf], axis=1)
            ka_neg = jnp.concatenate([k, -kf], axis=1)
        else:
            ka_any = jnp.concatenate([k, jnp.where(keys_after, -kf, kf)], axis=1)
        chunks = [(kk, c, n) for kk in range(tk // DIFF_KCHUNK) for c in range(2)
                  for n in range(tq // DIFF_QCHUNK)]

        def slices(kk, c, n):
            return (slice(kk * DIFF_KCHUNK, (kk + 1) * DIFF_KCHUNK),
                    slice(n * DIFF_QCHUNK, (n + 1) * DIFF_QCHUNK))

        def scores(kk, c, n):
            ks, cs = slices(kk, c, n)
            qa = qa_ref[c, cs, :]
            if not diag:
                return lax.dot_general(ka_any[ks], qa, nt, preferred_element_type=F32)
            k_lo, q_lo = kk * DIFF_KCHUNK, n * DIFF_QCHUNK
            if k_lo + DIFF_KCHUNK - 1 <= q_lo:
                return lax.dot_general(ka_pos[ks], qa, nt, preferred_element_type=F32)
            if k_lo >= q_lo + DIFF_QCHUNK - 1:
                return lax.dot_general(ka_neg[ks], qa, nt, preferred_element_type=F32)
            off = q_lo - k_lo
            st = lax.dot_general(ka_pos[ks], qa, nt, preferred_element_type=F32)
            return st + corr_ref[:, off:off + DIFF_QCHUNK]

        def softmax(st, kk, c, n):
            _, cs = slices(kk, c, n)
            m_prev = m_ref[c, :, cs]
            m_new = jnp.maximum(m_prev, jnp.max(st, axis=0, keepdims=True))
            alpha = jnp.exp2(m_prev - m_new)
            p = jnp.exp2(st - m_new)
            l_ref[c, :, cs] = alpha * l_ref[c, :, cs] + jnp.sum(p, axis=0, keepdims=True)
            m_ref[c, :, cs] = m_new
            return alpha, p.astype(BF16)

        def accumulate(alpha, p, kk, c, n):
            ks, cs = slices(kk, c, n)
            acc_ref[c, :, cs] = alpha * acc_ref[c, :, cs] + jnp.dot(vt[:, ks], p, preferred_element_type=F32)

        nc = len(chunks)
        st_q = {0: scores(*chunks[0])}
        if nc > 1:
            st_q[1] = scores(*chunks[1])
        p_q = {0: softmax(st_q.pop(0), *chunks[0])}
        for i in range(nc):
            if i + 2 < nc:
                st_q[i + 2] = scores(*chunks[i + 2])
            if i + 1 < nc:
                p_q[i + 1] = softmax(st_q.pop(i + 1), *chunks[i + 1])
            accumulate(*p_q.pop(i), *chunks[i])

    pl.when(qi != kj)(lambda: step(False))
    pl.when(qi == kj)(lambda: step(True))

    @pl.when(kj == pl.num_programs(3) - 1)
    def _():
        lv = lam_ref[...]
        lam = (jnp.exp(jnp.sum(lv[0:1] * lv[1:2], axis=-1, keepdims=True))
               - jnp.exp(jnp.sum(lv[2:3] * lv[3:4], axis=-1, keepdims=True)) + lambda_init)
        at = acc_ref[0] / l_ref[0] - lam * (acc_ref[1] / l_ref[1])
        ms = jnp.mean(at * at, axis=0, keepdims=True)
        y = (at * lax.rsqrt(ms + EPS)).T * sw_ref[...] * (1.0 - lambda_init)
        o_ref[...] = (y * _silu(z_ref[...].astype(F32))).astype(BF16)


def _diff_attention(proj, v_t, lam_vecs, subln_w, batch, seq, lambda_init, tq=2048, tk=2048):
    t = batch * seq
    nq = seq // tq
    nk = seq // tk
    hw = 2 * DIFF_HEAD_DIM
    assert tq == tk and DIFF_KCHUNK % DIFF_QCHUNK == 0, "diagonal handling assumes aligned square tiles"
    qfeat, kfeat = _alibi_features(seq)
    rel = jnp.arange(DIFF_KCHUNK, dtype=F32)[:, None] - jnp.arange(DIFF_KCHUNK, dtype=F32)[None, :]
    corr = -2.0 * jnp.asarray(_alibi_slopes(), F32)[:, None, None] * jnp.maximum(rel, 0.0)[None]
    kern = functools.partial(_diff_kernel, tq=tq, tk=tk, lambda_init=lambda_init)
    return pl.pallas_call(
        kern,
        grid=(batch, DIFF_HEADS, nq, nk),
        in_specs=[
            pl.BlockSpec((4, DIFF_HEAD_DIM), lambda b, h, i, j: (0, 0)),
            pl.BlockSpec((1, hw), lambda b, h, i, j: (0, 0)),
            pl.BlockSpec((tq, hw), lambda b, h, i, j: (b * nq + i, COL_DF_Q // hw + h)),
            pl.BlockSpec((None, tq, LANES), lambda b, h, i, j: (h, i, 0)),
            pl.BlockSpec((tk, hw), lambda b, h, i, j: (b * nk + j, COL_DF_K // hw + h)),
            pl.BlockSpec((None, tk, LANES), lambda b, h, i, j: (h, j, 0)),
            pl.BlockSpec((hw, tk), lambda b, h, i, j: (h, b * nk + j)),
            pl.BlockSpec((None, DIFF_KCHUNK, DIFF_KCHUNK), lambda b, h, i, j: (h, 0, 0)),
            pl.BlockSpec((tq, hw), lambda b, h, i, j: (b * nq + i, (COL_Z + BRANCH_W) // hw + h)),
        ],
        out_specs=pl.BlockSpec((tq, hw), lambda b, h, i, j: (b * nq + i, h)),
        out_shape=jax.ShapeDtypeStruct((t, BRANCH_W), BF16),
        scratch_shapes=[
            pltpu.VMEM((2, tq, 2 * hw), BF16),
            pltpu.VMEM((2, 1, tq), F32),
            pltpu.VMEM((2, 1, tq), F32),
            pltpu.VMEM((2, hw, tq), F32),
        ],
        compiler_params=_params(("parallel", "parallel", "parallel", "arbitrary")),
        name="diff_attention",
    )(lam_vecs, subln_w.reshape(1, hw), proj, qfeat, proj, kfeat, v_t, corr, proj)


POOL_PAD = 8


def _pool_kernel(x_ref, z_ref, w_ref, sc_ref, o_ref, pad_ref, *, seq):
    t = lax.broadcasted_iota(jnp.int32, (seq, POOL_GROUP_W), 0)
    zeros = jnp.zeros((POOL_PAD, POOL_GROUP_W), F32)
    pad_ref[0:POOL_PAD, :] = zeros
    pad_ref[POOL_PAD + seq:, :] = zeros
    for g, win in enumerate(POOL_WINDOWS):
        gs = slice(g * POOL_GROUP_W, (g + 1) * POOL_GROUP_W)
        x = x_ref[:, gs].astype(F32)
        pad_ref[POOL_PAD:POOL_PAD + seq, :] = x
        half = win // 2
        total = x
        for d in range(-half, win - half):
            if d != 0:
                total = total + pad_ref[POOL_PAD + d:POOL_PAD + d + seq, :]
        cnt = jnp.minimum(t + (win - half), seq) - jnp.maximum(t - half, 0)
        pooled = total / cnt.astype(F32) - x
        y = jnp.dot(pooled.astype(BF16), w_ref[g], preferred_element_type=F32) * sc_ref[:, gs]
        o_ref[:, gs] = (y * _silu(z_ref[:, gs].astype(F32))).astype(BF16)


def _pool_mixer(proj, pool_w_bf, pool_scale, batch, seq):
    t = batch * seq
    assert POOL_PAD >= max(POOL_WINDOWS) // 2
    return pl.pallas_call(
        functools.partial(_pool_kernel, seq=seq),
        grid=(batch,),
        in_specs=[
            pl.BlockSpec((seq, BRANCH_W), lambda b: (b, COL_POOL // BRANCH_W)),
            pl.BlockSpec((seq, BRANCH_W), lambda b: (b, COL_Z // BRANCH_W + 2)),
            pl.BlockSpec((len(POOL_WINDOWS), POOL_GROUP_W, POOL_GROUP_W), lambda b: (0, 0, 0)),
            pl.BlockSpec((1, BRANCH_W), lambda b: (0, 0)),
        ],
        out_specs=pl.BlockSpec((seq, BRANCH_W), lambda b: (b, 0)),
        out_shape=jax.ShapeDtypeStruct((t, BRANCH_W), BF16),
        scratch_shapes=[pltpu.VMEM((seq + 2 * POOL_PAD, POOL_GROUP_W), F32)],
        compiler_params=_params(("parallel",)),
        name="pool_mixer",
    )(proj, proj, pool_w_bf, pool_scale.reshape(1, BRANCH_W))


def _sgu_kernel(u_ref, v_ref, z_ref, lw_ref, lb_ref, ws_ref, bs_ref, o_ref, *, tm):
    v = v_ref[...].astype(F32)
    mu = jnp.mean(v, axis=-1, keepdims=True)
    vc = v - mu
    var = jnp.mean(vc * vc, axis=-1, keepdims=True)
    vn = (vc * lax.rsqrt(var + EPS) * lw_ref[...] + lb_ref[...]).astype(BF16)
    for n in range(tm // SGU_CHUNK):
        rs = slice(n * SGU_CHUNK, (n + 1) * SGU_CHUNK)
        for g in range(SGU_GROUPS):
            gs = slice(g * SGU_GROUP_W, (g + 1) * SGU_GROUP_W)
            mixed = jnp.dot(ws_ref[g], vn[rs, gs], preferred_element_type=F32) + bs_ref[g]
            y = u_ref[rs, gs].astype(F32) * mixed
            o_ref[rs, gs] = (y * _silu(z_ref[rs, gs].astype(F32))).astype(BF16)


def _sgu_mixer(proj, ln_w, ln_b, ws_bf, bs_full, t, tm=2048):
    return pl.pallas_call(
        functools.partial(_sgu_kernel, tm=tm),
        grid=(t // tm,),
        in_specs=[
            pl.BlockSpec((tm, BRANCH_W), lambda i: (i, COL_SGU_U // BRANCH_W)),
            pl.BlockSpec((tm, BRANCH_W), lambda i: (i, COL_SGU_V // BRANCH_W)),
            pl.BlockSpec((tm, BRANCH_W), lambda i: (i, COL_Z // BRANCH_W + 3)),
            pl.BlockSpec((1, BRANCH_W), lambda i: (0, 0)),
            pl.BlockSpec((1, BRANCH_W), lambda i: (0, 0)),
            pl.BlockSpec((SGU_GROUPS, SGU_CHUNK, SGU_CHUNK), lambda i: (0, 0, 0)),
            pl.BlockSpec((SGU_GROUPS, SGU_CHUNK, SGU_GROUP_W), lambda i: (0, 0, 0)),
        ],
        out_specs=pl.BlockSpec((tm, BRANCH_W), lambda i: (i, 0)),
        out_shape=jax.ShapeDtypeStruct((t, BRANCH_W), BF16),
        compiler_params=_params(("parallel",)),
        name="sgu_mixer",
    )(proj, proj, proj, ln_w.reshape(1, BRANCH_W), ln_b.reshape(1, BRANCH_W), ws_bf, bs_full)


def _merge_kernel(h_ref, ya_ref, yb_ref, yc_ref, yd_ref, wg_ref, wb_ref, o_ref):
    h = h_ref[...]
    merged = None
    for i, y_ref in enumerate((ya_ref, yb_ref, yc_ref, yd_ref)):
        gate = _sigmoid(jnp.dot(h, wg_ref[i], preferred_element_type=F32))
        term = gate * jnp.dot(y_ref[...], wb_ref[i], preferred_element_type=F32)
        merged = term if merged is None else merged + term
    o_ref[...] = merged.astype(BF16)


def _merge(h, ys, wg_bf, wb_bf, tm=1024, tn=512):
    t = h.shape[0]
    y_spec = pl.BlockSpec((tm, BRANCH_W), lambda i, j: (i, 0))
    return pl.pallas_call(
        _merge_kernel,
        grid=(t // tm, D_MODEL // tn),
        in_specs=[
            pl.BlockSpec((tm, D_MODEL), lambda i, j: (i, 0)),
            y_spec, y_spec, y_spec, y_spec,
            pl.BlockSpec((N_BRANCH, D_MODEL, tn), lambda i, j: (0, 0, j)),
            pl.BlockSpec((N_BRANCH, BRANCH_W, tn), lambda i, j: (0, 0, j)),
        ],
        out_specs=pl.BlockSpec((tm, tn), lambda i, j: (i, j)),
        out_shape=jax.ShapeDtypeStruct((t, D_MODEL), BF16),
        compiler_params=_params(("parallel", "arbitrary")),
        name="merge",
    )(h, *ys, wg_bf, wb_bf)


def _outproj_kernel(x_ref, m_ref, w_ref, fw_ref, o_ref, *, final_norm):
    x = x_ref[...] + jnp.dot(m_ref[...], w_ref[...], preferred_element_type=F32)
    if final_norm:
        ms = jnp.mean(x * x, axis=-1, keepdims=True)
        x = x * lax.rsqrt(ms + EPS) * fw_ref[...]
    o_ref[...] = x


def _outproj(x2, merged, w_out_bf, final_w, final_norm, tm=512):
    t = x2.shape[0]
    return pl.pallas_call(
        functools.partial(_outproj_kernel, final_norm=final_norm),
        grid=(t // tm,),
        in_specs=[
            pl.BlockSpec((tm, D_MODEL), lambda i: (i, 0)),
            pl.BlockSpec((tm, D_MODEL), lambda i: (i, 0)),
            pl.BlockSpec((D_MODEL, D_MODEL), lambda i: (0, 0)),
            pl.BlockSpec((1, D_MODEL), lambda i: (0, 0)),
        ],
        out_specs=pl.BlockSpec((tm, D_MODEL), lambda i: (i, 0)),
        out_shape=jax.ShapeDtypeStruct((t, D_MODEL), F32),
        compiler_params=_params(("parallel",)),
        name="outproj",
    )(x2, merged, w_out_bf, final_w.reshape(1, D_MODEL))


def _column_scale():
    cs = jnp.ones((IN_COLS,), F32)
    cs = cs.at[COL_NA_Q:COL_NA_Q + BRANCH_W].set(NA_HEAD_DIM ** -0.5 * LOG2E)
    cs = cs.at[COL_DF_Q:COL_DF_Q + BRANCH_W].set(DIFF_HEAD_DIM ** -0.5 * LOG2E)
    return cs.reshape(1, IN_COLS)


def kernel(x, norm_w, w_in, na_bias, lam_q1, lam_k1, lam_q2, lam_k2, diff_subln_w, pool_w, pool_scale,
           sgu_ln_w, sgu_ln_b, sgu_ws, sgu_b, w_branch, w_gate, w_out, final_norm_w):
    batch, seq, _ = x.shape
    depth = norm_w.shape[0]
    t = batch * seq
    rows = seq // GRID_W
    x2 = x.reshape(t, D_MODEL)
    col_scale = _column_scale()
    pat_ids, row_sel, col_sel = _na_patterns(rows)
    for l in range(depth):
        h, proj, vt_na, vt_df = _inproj(x2, norm_w[l], w_in[l].astype(BF16), col_scale)
        y_a = _na_attention(proj, vt_na, _na_dense_bias(na_bias[l], row_sel, col_sel), pat_ids, batch, seq)
        lambda_init = 0.8 - 0.6 * math.exp(-0.3 * l)
        lam_vecs = jnp.stack([lam_q1[l], lam_k1[l], lam_q2[l], lam_k2[l]]).astype(F32)
        y_b = _diff_attention(proj, vt_df, lam_vecs, diff_subln_w[l], batch, seq, lambda_init)
        y_c = _pool_mixer(proj, pool_w[l].astype(BF16), pool_scale[l], batch, seq)
        bs_full = jnp.broadcast_to(sgu_b[l][:, :, None], (SGU_GROUPS, SGU_CHUNK, SGU_GROUP_W))
        y_d = _sgu_mixer(proj, sgu_ln_w[l], sgu_ln_b[l], sgu_ws[l].astype(BF16), bs_full, t)
        merged = _merge(h, (y_a, y_b, y_c, y_d), w_gate[l].astype(BF16), w_branch[l].astype(BF16))
        x2 = _outproj(x2, merged, w_out[l].astype(BF16), final_norm_w, l == depth - 1)
    return x2.reshape(batch, seq, D_MODEL)
```

```python
import functools
import math

import jax
import jax.numpy as jnp
import numpy as np
from jax import lax
from jax.experimental import pallas as pl
from jax.experimental.pallas import tpu as pltpu

D_MODEL = 2048
GRID_W = 64
N_BRANCH = 4
BRANCH_W = D_MODEL // N_BRANCH
NA_HEADS = 4
NA_HEAD_DIM = BRANCH_W // NA_HEADS
NA_ROWS_MAX = 8
NA_COLS = 16
DIFF_HEADS = 4
DIFF_HEAD_DIM = BRANCH_W // (2 * DIFF_HEADS)
POOL_WINDOWS = (2, 4, 8, 16)
POOL_GROUP_W = BRANCH_W // len(POOL_WINDOWS)
SGU_CHUNK = 128
SGU_GROUPS = 4
SGU_GROUP_W = BRANCH_W // SGU_GROUPS
IN_COLS = 13 * BRANCH_W
EPS = 1e-6

COL_NA_Q = 0
COL_NA_K = BRANCH_W
COL_NA_V = 2 * BRANCH_W
COL_DF_Q = 3 * BRANCH_W
COL_DF_K = 4 * BRANCH_W
COL_DF_V = 5 * BRANCH_W
COL_POOL = 6 * BRANCH_W
COL_SGU_U = 7 * BRANCH_W
COL_SGU_V = 8 * BRANCH_W
COL_Z = 9 * BRANCH_W

LANES = 128
VMEM_LIMIT = 56 * 1024 * 1024
MASK_VALUE = -1e30
LOG2E = math.log2(math.e)

BF16 = jnp.bfloat16
F32 = jnp.float32


def _params(sem, vmem=VMEM_LIMIT):
    return pltpu.CompilerParams(dimension_semantics=sem, vmem_limit_bytes=vmem)


def _silu(z):
    return z * (1.0 / (1.0 + jnp.exp(-z)))


def _sigmoid(z):
    return 1.0 / (1.0 + jnp.exp(-z))


def _inproj_kernel(x_ref, nw_ref, w_ref, cs_ref, h_ref, p_ref, vta_ref, vtd_ref, *, vt_places):
    @pl.when(pl.program_id(1) == 0)
    def _():
        x = x_ref[...]
        ms = jnp.mean(x * x, axis=-1, keepdims=True)
        h_ref[...] = (x * lax.rsqrt(ms + EPS) * nw_ref[...]).astype(BF16)

    acc = jnp.dot(h_ref[...], w_ref[...], preferred_element_type=F32) * cs_ref[...]
    p_ref[...] = acc.astype(BF16)

    for vt_ref, (tile, off) in zip((vta_ref, vtd_ref), vt_places):
        @pl.when(pl.program_id(1) == tile)
        def _(vt_ref=vt_ref, off=off):
            vt_ref[...] = acc[:, off:off + BRANCH_W].T.astype(BF16)


def _inproj(x2, norm_w, w_in_bf, col_scale, tm=512, tn=IN_COLS // 2):
    t = x2.shape[0]
    vt_places = tuple(divmod(col, tn) for col in (COL_NA_V, COL_DF_V))
    for _, off in vt_places:
        assert off + BRANCH_W <= tn and off % LANES == 0, "value columns must sit inside one column tile"
    vt_spec = pl.BlockSpec((BRANCH_W, tm), lambda i, j: (0, i))
    vt_shape = jax.ShapeDtypeStruct((BRANCH_W, t), BF16)
    return pl.pallas_call(
        functools.partial(_inproj_kernel, vt_places=vt_places),
        grid=(t // tm, IN_COLS // tn),
        in_specs=[
            pl.BlockSpec((tm, D_MODEL), lambda i, j: (i, 0)),
            pl.BlockSpec((1, D_MODEL), lambda i, j: (0, 0)),
            pl.BlockSpec((D_MODEL, tn), lambda i, j: (0, j)),
            pl.BlockSpec((1, tn), lambda i, j: (0, j)),
        ],
        out_specs=[
            pl.BlockSpec((tm, D_MODEL), lambda i, j: (i, 0)),
            pl.BlockSpec((tm, tn), lambda i, j: (i, j)),
            vt_spec, vt_spec,
        ],
        out_shape=[
            jax.ShapeDtypeStruct((t, D_MODEL), BF16),
            jax.ShapeDtypeStruct((t, IN_COLS), BF16),
            vt_shape, vt_shape,
        ],
        compiler_params=_params(("parallel", "arbitrary")),
        name="inproj",
    )(x2, norm_w.reshape(1, D_MODEL), w_in_bf, col_scale)


NA_QROWS = 8
NA_KROWS = 16
NA_TQ = NA_QROWS * GRID_W
NA_TK = NA_KROWS * GRID_W
NA_KSPLIT = 4
NA_KBLK = NA_TK // NA_KSPLIT


def _na_window_start(rb, rows):
    lo = rb * NA_QROWS - NA_ROWS_MAX // 2
    return jnp.clip(lo, 0, rows - NA_KROWS)


def _na_patterns(rows):
    nrb = rows // NA_QROWS
    kr_win = min(NA_ROWS_MAX, rows)
    n_roff = 2 * NA_ROWS_MAX - 1
    n_coff = 2 * NA_COLS - 1
    pats, pat_ids = [], []
    for rb in range(nrb):
        r = rb * NA_QROWS + np.arange(NA_QROWS)
        start = int(np.clip(rb * NA_QROWS - NA_ROWS_MAX // 2, 0, rows - NA_KROWS))
        kr = start + np.arange(NA_KROWS)
        rs = np.clip(r - kr_win // 2, 0, rows - kr_win)
        ok = (kr[None, :] >= rs[:, None]) & (kr[None, :] < rs[:, None] + kr_win)
        off = kr[None, :] - r[:, None] + (NA_ROWS_MAX - 1)
        sel = np.zeros((NA_QROWS, NA_KROWS, n_roff + 1), np.float32)
        for i in range(NA_QROWS):
            for j in range(NA_KROWS):
                sel[i, j, off[i, j] if ok[i, j] else n_roff] = 1.0
        for pid, p in enumerate(pats):
            if np.array_equal(p, sel):
                pat_ids.append(pid)
                break
        else:
            pat_ids.append(len(pats))
            pats.append(sel)
    c = np.arange(GRID_W)
    cs = np.clip(c - NA_COLS // 2, 0, GRID_W - NA_COLS)
    col_sel = np.zeros((GRID_W, GRID_W, n_coff), np.float32)
    for qc in range(GRID_W):
        for kc in range(cs[qc], cs[qc] + NA_COLS):
            col_sel[qc, kc, kc - qc + (NA_COLS - 1)] = 1.0
    return pat_ids, np.stack(pats), col_sel


def _na_dense_bias(rel_bias, row_sel, col_sel):
    npat = row_sel.shape[0]
    n_sel = row_sel.shape[-1]
    assert 2 * GRID_W == LANES
    cols = jnp.einsum("hab,qkb->hakq", rel_bias.astype(F32), col_sel, precision=lax.Precision.HIGHEST) * LOG2E
    cols = jnp.where((col_sel.sum(-1) > 0).T[None, None], cols, MASK_VALUE)
    cols = jnp.concatenate([cols, jnp.full((NA_HEADS, 1, GRID_W, GRID_W), MASK_VALUE, F32)], axis=1)
    cols2 = jnp.concatenate([cols, cols], axis=-1)
    tile_ids = jnp.asarray(np.argmax(row_sel, axis=-1).reshape(-1), jnp.int32)

    def body(ids_ref, cols_ref, o_ref):
        base = pl.program_id(0) * (NA_QROWS * NA_KROWS)
        left = lax.broadcasted_iota(jnp.int32, (GRID_W, LANES), 1) < GRID_W
        for j in range(NA_KROWS):
            for ii in range(NA_QROWS // 2):
                even = cols_ref[ids_ref[base + 2 * ii * NA_KROWS + j]]
                odd = cols_ref[ids_ref[base + (2 * ii + 1) * NA_KROWS + j]]
                o_ref[j * GRID_W:(j + 1) * GRID_W, ii * LANES:(ii + 1) * LANES] = jnp.where(left, even, odd)

    return pl.pallas_call(
        body,
        grid_spec=pltpu.PrefetchScalarGridSpec(
            num_scalar_prefetch=1,
            grid=(npat, NA_HEADS),
            in_specs=[pl.BlockSpec((None, n_sel, GRID_W, LANES), lambda p, h, ids: (h, 0, 0, 0))],
            out_specs=pl.BlockSpec((None, None, NA_TK, NA_TQ), lambda p, h, ids: (p, h, 0, 0)),
        ),
        out_shape=jax.ShapeDtypeStruct((npat, NA_HEADS, NA_TK, NA_TQ), F32),
        compiler_params=_params(("parallel", "parallel")),
        name="na_bias_expand",
    )(tile_ids, cols2)


def _na_kernel(q_ref, k0, k1, k2, k3, v0, v1, v2, v3, z_ref, b_ref, o_ref):
    nt = (((1,), (1,)), ((), ()))
    k_all = jnp.concatenate([r[...] for r in (k0, k1, k2, k3)], axis=0)
    vt_all = jnp.concatenate([r[...] for r in (v0, v1, v2, v3)], axis=1)

    def head(h):
        return slice(h * NA_HEAD_DIM, (h + 1) * NA_HEAD_DIM)

    def scores(h):
        st = lax.dot_general(k_all[:, head(h)], q_ref[:, head(h)], nt, preferred_element_type=F32)
        return st + b_ref[h]

    def softmax(st):
        p = jnp.exp2(st - jnp.max(st, axis=0, keepdims=True))
        return jnp.sum(p, axis=0, keepdims=True), p.astype(BF16)

    def output(l, p, h):
        ot = jnp.dot(vt_all[head(h)], p, preferred_element_type=F32) / l
        o_ref[:, head(h)] = (ot.T * _silu(z_ref[:, head(h)].astype(F32))).astype(BF16)

    st_next = scores(0)
    pending = None
    for h in range(NA_HEADS):
        st = st_next
        if h + 1 < NA_HEADS:
            st_next = scores(h + 1)
        lp = softmax(st)
        if pending is not None:
            output(*pending)
        pending = (*lp, h)
    output(*pending)


def _na_attention(proj, v_t, dense_bias, pat_ids, batch, seq):
    rows = seq // GRID_W
    nrb = rows // NA_QROWS
    t = batch * seq
    qblk_per_seq = seq // NA_TQ
    kblk_per_seq = seq // NA_KBLK
    kblk_per_row = NA_KBLK // GRID_W

    def q_map(col):
        return lambda rb, b: (b * qblk_per_seq + rb, col)

    def k_map(j):
        def f(rb, b):
            start = _na_window_start(rb, rows) // kblk_per_row
            return (b * kblk_per_seq + start + j, COL_NA_K // BRANCH_W)
        return f

    def vt_map(j):
        return lambda rb, b: (0, k_map(j)(rb, b)[0])

    def bias_map(rb, b):
        pid = 0
        for r, p in enumerate(pat_ids):
            pid = pid + jnp.where(rb == r, p, 0)
        return (pid, 0, 0, 0)

    in_specs = [pl.BlockSpec((NA_TQ, BRANCH_W), q_map(COL_NA_Q // BRANCH_W))]
    in_specs += [pl.BlockSpec((NA_KBLK, BRANCH_W), k_map(j)) for j in range(NA_KSPLIT)]
    in_specs += [pl.BlockSpec((BRANCH_W, NA_KBLK), vt_map(j)) for j in range(NA_KSPLIT)]
    in_specs += [pl.BlockSpec((NA_TQ, BRANCH_W), q_map(COL_Z // BRANCH_W + 0))]
    in_specs += [pl.BlockSpec((None, NA_HEADS, NA_TK, NA_TQ), bias_map)]
    return pl.pallas_call(
        _na_kernel,
        grid=(nrb, batch),
        in_specs=in_specs,
        out_specs=pl.BlockSpec((NA_TQ, BRANCH_W), lambda rb, b: (b * qblk_per_seq + rb, 0)),
        out_shape=jax.ShapeDtypeStruct((t, BRANCH_W), BF16),
        compiler_params=_params(("parallel", "parallel")),
        name="na_attention",
    )(*([proj] * 5), *([v_t] * 4), proj, dense_bias)


DIFF_QCHUNK = 512
DIFF_KCHUNK = 1024


def _alibi_slopes():
    return [2.0 ** (-8.0 * (h + 1) / DIFF_HEADS) * LOG2E for h in range(DIFF_HEADS)]


def _split3(x):
    def trunc(v):
        return (v.view(np.uint32) & np.uint32(0xFFFF0000)).view(np.float32)
    x1 = trunc(x)
    r1 = x - x1
    x2 = trunc(r1)
    return x1, x2, r1 - x2


def _alibi_features(seq):
    pos = np.arange(seq, dtype=np.float32)
    slopes = np.asarray(_alibi_slopes(), np.float32)[:, None]
    u = _split3(-slopes * pos[None, :])
    w = _split3(slopes * pos[None, :])
    one = np.ones((DIFF_HEADS, seq), np.float32)
    zero = np.zeros((DIFF_HEADS, seq, LANES - 6), np.float32)
    qfeat = np.concatenate([np.stack(list(u) + [one] * 3, axis=-1), zero], axis=-1)
    kfeat = np.concatenate([np.stack([one] * 3 + list(w), axis=-1), zero], axis=-1)
    return jnp.asarray(qfeat).astype(BF16), jnp.asarray(kfeat).astype(BF16)


def _diff_kernel(lam_ref, sw_ref, q_ref, qf_ref, k_ref, kf_ref, vt_ref, corr_ref, z_ref, o_ref,
                 qa_ref, m_ref, l_ref, acc_ref, *, tq, tk, lambda_init):
    qi = pl.program_id(2)
    kj = pl.program_id(3)
    nt = (((1,), (1,)), ((), ()))
    hw = 2 * DIFF_HEAD_DIM

    @pl.when(kj == 0)
    def _():
        m_ref[...] = jnp.full(m_ref.shape, MASK_VALUE, F32)
        l_ref[...] = jnp.zeros(l_ref.shape, F32)
        acc_ref[...] = jnp.zeros(acc_ref.shape, F32)
        q = q_ref[...]
        lane = lax.broadcasted_iota(jnp.int32, q.shape, 1)
        for c in range(2):
            in_c = (lane >= c * DIFF_HEAD_DIM) & (lane < (c + 1) * DIFF_HEAD_DIM)
            qa_ref[c, :, :hw] = jnp.where(in_c, q, jnp.zeros_like(q))
            qa_ref[c, :, hw:] = qf_ref[...]

    keys_after = kj > qi
    k = k_ref[...]
    kf = kf_ref[...]
    vt = vt_ref[...]

    def step(diag):
        if diag:
            ka_pos = jnp.concatenate([k, kf], axis=1)
            ka_neg = jnp.concatenate([k, -kf], axis=1)
        else:
            ka_any = jnp.concatenate([k, jnp.where(keys_after, -kf, kf)], axis=1)
        chunks = [(kk, c, n) for kk in range(tk // DIFF_KCHUNK) for c in range(2)
                  for n in range(tq // DIFF_QCHUNK)]

        def slices(kk, c, n):
            return (slice(kk * DIFF_KCHUNK, (kk + 1) * DIFF_KCHUNK),
                    slice(n * DIFF_QCHUNK, (n + 1) * DIFF_QCHUNK))

        def scores(kk, c, n):
            ks, cs = slices(kk, c, n)
            qa = qa_ref[c, cs, :]
            if not diag:
                return lax.dot_general(ka_any[ks], qa, nt, preferred_element_type=F32)
            k_lo, q_lo = kk * DIFF_KCHUNK, n * DIFF_QCHUNK
            if k_lo + DIFF_KCHUNK - 1 <= q_lo:
                return lax.dot_general(ka_pos[ks], qa, nt, preferred_element_type=F32)
            if k_lo >= q_lo + DIFF_QCHUNK - 1:
                return lax.dot_general(ka_neg[ks], qa, nt, preferred_element_type=F32)
            off = q_lo - k_lo
            st = lax.dot_general(ka_pos[ks], qa, nt, preferred_element_type=F32)
            return st + corr_ref[:, off:off + DIFF_QCHUNK]

        def softmax(st, kk, c, n):
            _, cs = slices(kk, c, n)
            m_prev = m_ref[c, :, cs]
            m_new = jnp.maximum(m_prev, jnp.max(st, axis=0, keepdims=True))
            alpha = jnp.exp2(m_prev - m_new)
            p = jnp.exp2(st - m_new)
            l_ref[c, :, cs] = alpha * l_ref[c, :, cs] + jnp.sum(p, axis=0, keepdims=True)
            m_ref[c, :, cs] = m_new
            return alpha, p.astype(BF16)

        def accumulate(alpha, p, kk, c, n):
            ks, cs = slices(kk, c, n)
            acc_ref[c, :, cs] = alpha * acc_ref[c, :, cs] + jnp.dot(vt[:, ks], p, preferred_element_type=F32)

        nc = len(chunks)
        st_q = {0: scores(*chunks[0])}
        if nc > 1:
            st_q[1] = scores(*chunks[1])
        p_q = {0: softmax(st_q.pop(0), *chunks[0])}
        for i in range(nc):
            if i + 2 < nc:
                st_q[i + 2] = scores(*chunks[i + 2])
            if i + 1 < nc:
                p_q[i + 1] = softmax(st_q.pop(i + 1), *chunks[i + 1])
            accumulate(*p_q.pop(i), *chunks[i])

    pl.when(qi != kj)(lambda: step(False))
    pl.when(qi == kj)(lambda: step(True))

    @pl.when(kj == pl.num_programs(3) - 1)
    def _():
        lv = lam_ref[...]
        lam = (jnp.exp(jnp.sum(lv[0:1] * lv[1:2], axis=-1, keepdims=True))
               - jnp.exp(jnp.sum(lv[2:3] * lv[3:4], axis=-1, keepdims=True)) + lambda_init)
        at = acc_ref[0] / l_ref[0] - lam * (acc_ref[1] / l_ref[1])
        ms = jnp.mean(at * at, axis=0, keepdims=True)
        y = (at * lax.rsqrt(ms + EPS)).T * sw_ref[...] * (1.0 - lambda_init)
        o_ref[...] = (y * _silu(z_ref[...].astype(F32))).astype(BF16)


def _diff_attention(proj, v_t, lam_vecs, subln_w, batch, seq, lambda_init, tq=2048, tk=2048):
    t = batch * seq
    nq = seq // tq
    nk = seq // tk
    hw = 2 * DIFF_HEAD_DIM
    assert tq == tk and DIFF_KCHUNK % DIFF_QCHUNK == 0, "diagonal handling assumes aligned square tiles"
    qfeat, kfeat = _alibi_features(seq)
    rel = jnp.arange(DIFF_KCHUNK, dtype=F32)[:, None] - jnp.arange(DIFF_KCHUNK, dtype=F32)[None, :]
    corr = -2.0 * jnp.asarray(_alibi_slopes(), F32)[:, None, None] * jnp.maximum(rel, 0.0)[None]
    kern = functools.partial(_diff_kernel, tq=tq, tk=tk, lambda_init=lambda_init)
    return pl.pallas_call(
        kern,
        grid=(batch, DIFF_HEADS, nq, nk),
        in_specs=[
            pl.BlockSpec((4, DIFF_HEAD_DIM), lambda b, h, i, j: (0, 0)),
            pl.BlockSpec((1, hw), lambda b, h, i, j: (0, 0)),
            pl.BlockSpec((tq, hw), lambda b, h, i, j: (b * nq + i, COL_DF_Q // hw + h)),
            pl.BlockSpec((None, tq, LANES), lambda b, h, i, j: (h, i, 0)),
            pl.BlockSpec((tk, hw), lambda b, h, i, j: (b * nk + j, COL_DF_K // hw + h)),
            pl.BlockSpec((None, tk, LANES), lambda b, h, i, j: (h, j, 0)),
            pl.BlockSpec((hw, tk), lambda b, h, i, j: (h, b * nk + j)),
            pl.BlockSpec((None, DIFF_KCHUNK, DIFF_KCHUNK), lambda b, h, i, j: (h, 0, 0)),
            pl.BlockSpec((tq, hw), lambda b, h, i, j: (b * nq + i, (COL_Z + BRANCH_W) // hw + h)),
        ],
        out_specs=pl.BlockSpec((tq, hw), lambda b, h, i, j: (b * nq + i, h)),
        out_shape=jax.ShapeDtypeStruct((t, BRANCH_W), BF16),
        scratch_shapes=[
            pltpu.VMEM((2, tq, 2 * hw), BF16),
            pltpu.VMEM((2, 1, tq), F32),
            pltpu.VMEM((2, 1, tq), F32),
            pltpu.VMEM((2, hw, tq), F32),
        ],
        compiler_params=_params(("parallel", "parallel", "parallel", "arbitrary")),
        name="diff_attention",
    )(lam_vecs, subln_w.reshape(1, hw), proj, qfeat, proj, kfeat, v_t, corr, proj)


POOL_PAD = 8


def _pool_kernel(x_ref, z_ref, w_ref, sc_ref, o_ref, pad_ref, *, seq):
    t = lax.broadcasted_iota(jnp.int32, (seq, POOL_GROUP_W), 0)
    zeros = jnp.zeros((POOL_PAD, POOL_GROUP_W), F32)
    pad_ref[0:POOL_PAD, :] = zeros
    pad_ref[POOL_PAD + seq:, :] = zeros
    for g, win in enumerate(POOL_WINDOWS):
        gs = slice(g * POOL_GROUP_W, (g + 1) * POOL_GROUP_W)
        x = x_ref[:, gs].astype(F32)
        pad_ref[POOL_PAD:POOL_PAD + seq, :] = x
        half = win // 2
        total = x
        for d in range(-half, win - half):
            if d != 0:
                total = total + pad_ref[POOL_PAD + d:POOL_PAD + d + seq, :]
        cnt = jnp.minimum(t + (win - half), seq) - jnp.maximum(t - half, 0)
        pooled = total / cnt.astype(F32) - x
        y = jnp.dot(pooled.astype(BF16), w_ref[g], preferred_element_type=F32) * sc_ref[:, gs]
        o_ref[:, gs] = (y * _silu(z_ref[:, gs].astype(F32))).astype(BF16)


def _pool_mixer(proj, pool_w_bf, pool_scale, batch, seq):
    t = batch * seq
    assert POOL_PAD >= max(POOL_WINDOWS) // 2
    return pl.pallas_call(
        functools.partial(_pool_kernel, seq=seq),
        grid=(batch,),
        in_specs=[
            pl.BlockSpec((seq, BRANCH_W), lambda b: (b, COL_POOL // BRANCH_W)),
            pl.BlockSpec((seq, BRANCH_W), lambda b: (b, COL_Z // BRANCH_W + 2)),
            pl.BlockSpec((len(POOL_WINDOWS), POOL_GROUP_W, POOL_GROUP_W), lambda b: (0, 0, 0)),
            pl.BlockSpec((1, BRANCH_W), lambda b: (0, 0)),
        ],
        out_specs=pl.BlockSpec((seq, BRANCH_W), lambda b: (b, 0)),
        out_shape=jax.ShapeDtypeStruct((t, BRANCH_W), BF16),
        scratch_shapes=[pltpu.VMEM((seq + 2 * POOL_PAD, POOL_GROUP_W), F32)],
        compiler_params=_params(("parallel",)),
        name="pool_mixer",
    )(proj, proj, pool_w_bf, pool_scale.reshape(1, BRANCH_W))


def _sgu_kernel(u_ref, v_ref, z_ref, lw_ref, lb_ref, ws_ref, bs_ref, o_ref, *, tm):
    v = v_ref[...].astype(F32)
    mu = jnp.mean(v, axis=-1, keepdims=True)
    vc = v - mu
    var = jnp.mean(vc * vc, axis=-1, keepdims=True)
    vn = (vc * lax.rsqrt(var + EPS) * lw_ref[...] + lb_ref[...]).astype(BF16)
    for n in range(tm // SGU_CHUNK):
        rs = slice(n * SGU_CHUNK, (n + 1) * SGU_CHUNK)
        for g in range(SGU_GROUPS):
            gs = slice(g * SGU_GROUP_W, (g + 1) * SGU_GROUP_W)
            mixed = jnp.dot(ws_ref[g], vn[rs, gs], preferred_element_type=F32) + bs_ref[g]
            y = u_ref[rs, gs].astype(F32) * mixed
            o_ref[rs, gs] = (y * _silu(z_ref[rs, gs].astype(F32))).astype(BF16)


def _sgu_mixer(proj, ln_w, ln_b, ws_bf, bs_full, t, tm=2048):
    return pl.pallas_call(
        functools.partial(_sgu_kernel, tm=tm),
        grid=(t // tm,),
        in_specs=[
            pl.BlockSpec((tm, BRANCH_W), lambda i: (i, COL_SGU_U // BRANCH_W)),
            pl.BlockSpec((tm, BRANCH_W), lambda i: (i, COL_SGU_V // BRANCH_W)),
            pl.BlockSpec((tm, BRANCH_W), lambda i: (i, COL_Z // BRANCH_W + 3)),
            pl.BlockSpec((1, BRANCH_W), lambda i: (0, 0)),
            pl.BlockSpec((1, BRANCH_W), lambda i: (0, 0)),
            pl.BlockSpec((SGU_GROUPS, SGU_CHUNK, SGU_CHUNK), lambda i: (0, 0, 0)),
            pl.BlockSpec((SGU_GROUPS, SGU_CHUNK, SGU_GROUP_W), lambda i: (0, 0, 0)),
        ],
        out_specs=pl.BlockSpec((tm, BRANCH_W), lambda i: (i, 0)),
        out_shape=jax.ShapeDtypeStruct((t, BRANCH_W), BF16),
        compiler_params=_params(("parallel",)),
        name="sgu_mixer",
    )(proj, proj, proj, ln_w.reshape(1, BRANCH_W), ln_b.reshape(1, BRANCH_W), ws_bf, bs_full)


def _merge_kernel(h_ref, ya_ref, yb_ref, yc_ref, yd_ref, wg_ref, wb_ref, o_ref):
    h = h_ref[...]
    merged = None
    for i, y_ref in enumerate((ya_ref, yb_ref, yc_ref, yd_ref)):
        gate = _sigmoid(jnp.dot(h, wg_ref[i], preferred_element_type=F32))
        term = gate * jnp.dot(y_ref[...], wb_ref[i], preferred_element_type=F32)
        merged = term if merged is None else merged + term
    o_ref[...] = merged.astype(BF16)


def _merge(h, ys, wg_bf, wb_bf, tm=1024, tn=512):
    t = h.shape[0]
    y_spec = pl.BlockSpec((tm, BRANCH_W), lambda i, j: (i, 0))
    return pl.pallas_call(
        _merge_kernel,
        grid=(t // tm, D_MODEL // tn),
        in_specs=[
            pl.BlockSpec((tm, D_MODEL), lambda i, j: (i, 0)),
            y_spec, y_spec, y_spec, y_spec,
            pl.BlockSpec((N_BRANCH, D_MODEL, tn), lambda i, j: (0, 0, j)),
            pl.BlockSpec((N_BRANCH, BRANCH_W, tn), lambda i, j: (0, 0, j)),
        ],
        out_specs=pl.BlockSpec((tm, tn), lambda i, j: (i, j)),
        out_shape=jax.ShapeDtypeStruct((t, D_MODEL), BF16),
        compiler_params=_params(("parallel", "arbitrary")),
        name="merge",
    )(h, *ys, wg_bf, wb_bf)


def _outproj_kernel(x_ref, m_ref, w_ref, fw_ref, o_ref, *, final_norm):
    x = x_ref[...] + jnp.dot(m_ref[...], w_ref[...], preferred_element_type=F32)
    if final_norm:
        ms = jnp.mean(x * x, axis=-1, keepdims=True)
        x = x * lax.rsqrt(ms + EPS) * fw_ref[...]
    o_ref[...] = x


def _outproj(x2, merged, w_out_bf, final_w, final_norm, tm=512):
    t = x2.shape[0]
    return pl.pallas_call(
        functools.partial(_outproj_kernel, final_norm=final_norm),
        grid=(t // tm,),
        in_specs=[
            pl.BlockSpec((tm, D_MODEL), lambda i: (i, 0)),
            pl.BlockSpec((tm, D_MODEL), lambda i: (i, 0)),
            pl.BlockSpec((D_MODEL, D_MODEL), lambda i: (0, 0)),
            pl.BlockSpec((1, D_MODEL), lambda i: (0, 0)),
        ],
        out_specs=pl.BlockSpec((tm, D_MODEL), lambda i: (i, 0)),
        out_shape=jax.ShapeDtypeStruct((t, D_MODEL), F32),
        compiler_params=_params(("parallel",)),
        name="outproj",
    )(x2, merged, w_out_bf, final_w.reshape(1, D_MODEL))


def _column_scale():
    cs = jnp.ones((IN_COLS,), F32)
    cs = cs.at[COL_NA_Q:COL_NA_Q + BRANCH_W].set(NA_HEAD_DIM ** -0.5 * LOG2E)
    cs = cs.at[COL_DF_Q:COL_DF_Q + BRANCH_W].set(DIFF_HEAD_DIM ** -0.5 * LOG2E)
    return cs.reshape(1, IN_COLS)


def kernel(x, norm_w, w_in, na_bias, lam_q1, lam_k1, lam_q2, lam_k2, diff_subln_w, pool_w, pool_scale,
           sgu_ln_w, sgu_ln_b, sgu_ws, sgu_b, w_branch, w_gate, w_out, final_norm_w):
    batch, seq, _ = x.shape
    depth = norm_w.shape[0]
    t = batch * seq
    rows = seq // GRID_W
    x2 = x.reshape(t, D_MODEL)
    col_scale = _column_scale()
    pat_ids, row_sel, col_sel = _na_patterns(rows)
    for l in range(depth):
        h, proj, vt_na, vt_df = _inproj(x2, norm_w[l], w_in[l].astype(BF16), col_scale)
        y_a = _na_attention(proj, vt_na, _na_dense_bias(na_bias[l], row_sel, col_sel), pat_ids, batch, seq)
        lambda_init = 0.8 - 0.6 * math.exp(-0.3 * l)
        lam_vecs = jnp.stack([lam_q1[l], lam_k1[l], lam_q2[l], lam_k2[l]]).astype(F32)
        y_b = _diff_attention(proj, vt_df, lam_vecs, diff_subln_w[l], batch, seq, lambda_init)
        y_c = _pool_mixer(proj, pool_w[l].astype(BF16), pool_scale[l], batch, seq)
        bs_full = jnp.broadcast_to(sgu_b[l][:, :, None], (SGU_GROUPS, SGU_CHUNK, SGU_GROUP_W))
        y_d = _sgu_mixer(proj, sgu_ln_w[l], sgu_ln_b[l], sgu_ws[l].astype(BF16), bs_full, t)
        merged = _merge(h, (y_a, y_b, y_c, y_d), w_gate[l].astype(BF16), w_branch[l].astype(BF16))
        x2 = _outproj(x2, merged, w_out[l].astype(BF16), final_norm_w, l == depth - 1)
    return x2.reshape(batch, seq, D_MODEL)
```

```python
import functools
import math

import jax
import jax.numpy as jnp
import numpy as np
from jax import lax
from jax.experimental import pallas as pl
from jax.experimental.pallas import tpu as pltpu

D_MODEL = 2048
GRID_W = 64
N_BRANCH = 4
BRANCH_W = D_MODEL // N_BRANCH
NA_HEADS = 4
NA_HEAD_DIM = BRANCH_W // NA_HEADS
NA_ROWS_MAX = 8
NA_COLS = 16
DIFF_HEADS = 4
DIFF_HEAD_DIM = BRANCH_W // (2 * DIFF_HEADS)
POOL_WINDOWS = (2, 4, 8, 16)
POOL_GROUP_W = BRANCH_W // len(POOL_WINDOWS)
SGU_CHUNK = 128
SGU_GROUPS = 4
SGU_GROUP_W = BRANCH_W // SGU_GROUPS
IN_COLS = 13 * BRANCH_W
EPS = 1e-6

COL_NA_Q = 0
COL_NA_K = BRANCH_W
COL_NA_V = 2 * BRANCH_W
COL_DF_Q = 3 * BRANCH_W
COL_DF_K = 4 * BRANCH_W
COL_DF_V = 5 * BRANCH_W
COL_POOL = 6 * BRANCH_W
COL_SGU_U = 7 * BRANCH_W
COL_SGU_V = 8 * BRANCH_W
COL_Z = 9 * BRANCH_W

LANES = 128
VMEM_LIMIT = 56 * 1024 * 1024
MASK_VALUE = -1e30
LOG2E = math.log2(math.e)

BF16 = jnp.bfloat16
F32 = jnp.float32


def _params(sem, vmem=VMEM_LIMIT):
    return pltpu.CompilerParams(dimension_semantics=sem, vmem_limit_bytes=vmem)


def _silu(z):
    return z * (1.0 / (1.0 + jnp.exp(-z)))


def _sigmoid(z):
    return 1.0 / (1.0 + jnp.exp(-z))


def _inproj_kernel(x_ref, nw_ref, w_ref, cs_ref, h_ref, p_ref, vta_ref, vtd_ref, *, vt_places):
    @pl.when(pl.program_id(1) == 0)
    def _():
        x = x_ref[...]
        ms = jnp.mean(x * x, axis=-1, keepdims=True)
        h_ref[...] = (x * lax.rsqrt(ms + EPS) * nw_ref[...]).astype(BF16)

    acc = jnp.dot(h_ref[...], w_ref[...], preferred_element_type=F32) * cs_ref[...]
    p_ref[...] = acc.astype(BF16)

    for vt_ref, (tile, off) in zip((vta_ref, vtd_ref), vt_places):
        @pl.when(pl.program_id(1) == tile)
        def _(vt_ref=vt_ref, off=off):
            vt_ref[...] = acc[:, off:off + BRANCH_W].T.astype(BF16)


def _inproj(x2, norm_w, w_in_bf, col_scale, tm=512, tn=IN_COLS // 2):
    t = x2.shape[0]
    vt_places = tuple(divmod(col, tn) for col in (COL_NA_V, COL_DF_V))
    for _, off in vt_places:
        assert off + BRANCH_W <= tn and off % LANES == 0, "value columns must sit inside one column tile"
    vt_spec = pl.BlockSpec((BRANCH_W, tm), lambda i, j: (0, i))
    vt_shape = jax.ShapeDtypeStruct((BRANCH_W, t), BF16)
    return pl.pallas_call(
        functools.partial(_inproj_kernel, vt_places=vt_places),
        grid=(t // tm, IN_COLS // tn),
        in_specs=[
            pl.BlockSpec((tm, D_MODEL), lambda i, j: (i, 0)),
            pl.BlockSpec((1, D_MODEL), lambda i, j: (0, 0)),
            pl.BlockSpec((D_MODEL, tn), lambda i, j: (0, j)),
            pl.BlockSpec((1, tn), lambda i, j: (0, j)),
        ],
        out_specs=[
            pl.BlockSpec((tm, D_MODEL), lambda i, j: (i, 0)),
            pl.BlockSpec((tm, tn), lambda i, j: (i, j)),
            vt_spec, vt_spec,
        ],
        out_shape=[
            jax.ShapeDtypeStruct((t, D_MODEL), BF16),
            jax.ShapeDtypeStruct((t, IN_COLS), BF16),
            vt_shape, vt_shape,
        ],
        compiler_params=_params(("parallel", "arbitrary")),
        name="inproj",
    )(x2, norm_w.reshape(1, D_MODEL), w_in_bf, col_scale)


NA_QROWS = 8
NA_KROWS = 16
NA_TQ = NA_QROWS * GRID_W
NA_TK = NA_KROWS * GRID_W
NA_KSPLIT = 4
NA_KBLK = NA_TK // NA_KSPLIT


def _na_window_start(rb, rows):
    lo = rb * NA_QROWS - NA_ROWS_MAX // 2
    return jnp.clip(lo, 0, rows - NA_KROWS)


def _na_patterns(rows):
    nrb = rows // NA_QROWS
    kr_win = min(NA_ROWS_MAX, rows)
    n_roff = 2 * NA_ROWS_MAX - 1
    n_coff = 2 * NA_COLS - 1
    pats, pat_ids = [], []
    for rb in range(nrb):
        r = rb * NA_QROWS + np.arange(NA_QROWS)
        start = int(np.clip(rb * NA_QROWS - NA_ROWS_MAX // 2, 0, rows - NA_KROWS))
        kr = start + np.arange(NA_KROWS)
        rs = np.clip(r - kr_win // 2, 0, rows - kr_win)
        ok = (kr[None, :] >= rs[:, None]) & (kr[None, :] < rs[:, None] + kr_win)
        off = kr[None, :] - r[:, None] + (NA_ROWS_MAX - 1)
        sel = np.zeros((NA_QROWS, NA_KROWS, n_roff + 1), np.float32)
        for i in range(NA_QROWS):
            for j in range(NA_KROWS):
                sel[i, j, off[i, j] if ok[i, j] else n_roff] = 1.0
        for pid, p in enumerate(pats):
            if np.array_equal(p, sel):
                pat_ids.append(pid)
                break
        else:
            pat_ids.append(len(pats))
            pats.append(sel)
    c = np.arange(GRID_W)
    cs = np.clip(c - NA_COLS // 2, 0, GRID_W - NA_COLS)
    col_sel = np.zeros((GRID_W, GRID_W, n_coff), np.float32)
    for qc in range(GRID_W):
        for kc in range(cs[qc], cs[qc] + NA_COLS):
            col_sel[qc, kc, kc - qc + (NA_COLS - 1)] = 1.0
    return pat_ids, np.stack(pats), col_sel


def _na_dense_bias(rel_bias, row_sel, col_sel):
    npat = row_sel.shape[0]
    n_sel = row_sel.shape[-1]
    assert 2 * GRID_W == LANES
    cols = jnp.einsum("hab,qkb->hakq", rel_bias.astype(F32), col_sel, precision=lax.Precision.HIGHEST) * LOG2E
    cols = jnp.where((col_sel.sum(-1) > 0).T[None, None], cols, MASK_VALUE)
    cols = jnp.concatenate([cols, jnp.full((NA_HEADS, 1, GRID_W, GRID_W), MASK_VALUE, F32)], axis=1)
    cols2 = jnp.concatenate([cols, cols], axis=-1)
    tile_ids = jnp.asarray(np.argmax(row_sel, axis=-1).reshape(-1), jnp.int32)

    def body(ids_ref, cols_ref, o_ref):
        base = pl.program_id(0) * (NA_QROWS * NA_KROWS)
        left = lax.broadcasted_iota(jnp.int32, (GRID_W, LANES), 1) < GRID_W
        for j in range(NA_KROWS):
            for ii in range(NA_QROWS // 2):
                even = cols_ref[ids_ref[base + 2 * ii * NA_KROWS + j]]
                odd = cols_ref[ids_ref[base + (2 * ii + 1) * NA_KROWS + j]]
                o_ref[j * GRID_W:(j + 1) * GRID_W, ii * LANES:(ii + 1) * LANES] = jnp.where(left, even, odd)

    return pl.pallas_call(
        body,
        grid_spec=pltpu.PrefetchScalarGridSpec(
            num_scalar_prefetch=1,
            grid=(npat, NA_HEADS),
            in_specs=[pl.BlockSpec((None, n_sel, GRID_W, LANES), lambda p, h, ids: (h, 0, 0, 0))],
            out_specs=pl.BlockSpec((None, None, NA_TK, NA_TQ), lambda p, h, ids: (p, h, 0, 0)),
        ),
        out_shape=jax.ShapeDtypeStruct((npat, NA_HEADS, NA_TK, NA_TQ), F32),
        compiler_params=_params(("parallel", "parallel")),
        name="na_bias_expand",
    )(tile_ids, cols2)


def _na_kernel(q_ref, k0, k1, k2, k3, v0, v1, v2, v3, z_ref, b_ref, o_ref):
    nt = (((1,), (1,)), ((), ()))
    k_all = jnp.concatenate([r[...] for r in (k0, k1, k2, k3)], axis=0)
    vt_all = jnp.concatenate([r[...] for r in (v0, v1, v2, v3)], axis=1)

    def head(h):
        return slice(h * NA_HEAD_DIM, (h + 1) * NA_HEAD_DIM)

    def scores(h):
        st = lax.dot_general(k_all[:, head(h)], q_ref[:, head(h)], nt, preferred_element_type=F32)
        return st + b_ref[h]

    def softmax(st):
        p = jnp.exp2(st - jnp.max(st, axis=0, keepdims=True))
        return jnp.sum(p, axis=0, keepdims=True), p.astype(BF16)

    def output(l, p, h):
        ot = jnp.dot(vt_all[head(h)], p, preferred_element_type=F32) / l
        o_ref[:, head(h)] = (ot.T * _silu(z_ref[:, head(h)].astype(F32))).astype(BF16)

    st_next = scores(0)
    pending = None
    for h in range(NA_HEADS):
        st = st_next
        if h + 1 < NA_HEADS:
            st_next = scores(h + 1)
        lp = softmax(st)
        if pending is not None:
            output(*pending)
        pending = (*lp, h)
    output(*pending)


def _na_attention(proj, v_t, dense_bias, pat_ids, batch, seq):
    rows = seq // GRID_W
    nrb = rows // NA_QROWS
    t = batch * seq
    qblk_per_seq = seq // NA_TQ
    kblk_per_seq = seq // NA_KBLK
    kblk_per_row = NA_KBLK // GRID_W

    def q_map(col):
        return lambda rb, b: (b * qblk_per_seq + rb, col)

    def k_map(j):
        def f(rb, b):
            start = _na_window_start(rb, rows) // kblk_per_row
            return (b * kblk_per_seq + start + j, COL_NA_K // BRANCH_W)
        return f

    def vt_map(j):
        return lambda rb, b: (0, k_map(j)(rb, b)[0])

    def bias_map(rb, b):
        pid = 0
        for r, p in enumerate(pat_ids):
            pid = pid + jnp.where(rb == r, p, 0)
        return (pid, 0, 0, 0)

    in_specs = [pl.BlockSpec((NA_TQ, BRANCH_W), q_map(COL_NA_Q // BRANCH_W))]
    in_specs += [pl.BlockSpec((NA_KBLK, BRANCH_W), k_map(j)) for j in range(NA_KSPLIT)]
    in_specs += [pl.BlockSpec((BRANCH_W, NA_KBLK), vt_map(j)) for j in range(NA_KSPLIT)]
    in_specs += [pl.BlockSpec((NA_TQ, BRANCH_W), q_map(COL_Z // BRANCH_W + 0))]
    in_specs += [pl.BlockSpec((None, NA_HEADS, NA_TK, NA_TQ), bias_map)]
    return pl.pallas_call(
        _na_kernel,
        grid=(nrb, batch),
        in_specs=in_specs,
        out_specs=pl.BlockSpec((NA_TQ, BRANCH_W), lambda rb, b: (b * qblk_per_seq + rb, 0)),
        out_shape=jax.ShapeDtypeStruct((t, BRANCH_W), BF16),
        compiler_params=_params(("parallel", "parallel")),
        name="na_attention",
    )(*([proj] * 5), *([v_t] * 4), proj, dense_bias)


DIFF_QCHUNK = 512
DIFF_KCHUNK = 1024


def _alibi_slopes():
    return [2.0 ** (-8.0 * (h + 1) / DIFF_HEADS) * LOG2E for h in range(DIFF_HEADS)]


def _split3(x):
    def trunc(v):
        return (v.view(np.uint32) & np.uint32(0xFFFF0000)).view(np.float32)
    x1 = trunc(x)
    r1 = x - x1
    x2 = trunc(r1)
    return x1, x2, r1 - x2


def _alibi_features(seq):
    pos = np.arange(seq, dtype=np.float32)
    slopes = np.asarray(_alibi_slopes(), np.float32)[:, None]
    u = _split3(-slopes * pos[None, :])
    w = _split3(slopes * pos[None, :])
    one = np.ones((DIFF_HEADS, seq), np.float32)
    zero = np.zeros((DIFF_HEADS, seq, LANES - 6), np.float32)
    qfeat = np.concatenate([np.stack(list(u) + [one] * 3, axis=-1), zero], axis=-1)
    kfeat = np.concatenate([np.stack([one] * 3 + list(w), axis=-1), zero], axis=-1)
    return jnp.asarray(qfeat).astype(BF16), jnp.asarray(kfeat).astype(BF16)


def _diff_kernel(lam_ref, sw_ref, q_ref, qf_ref, k_ref, kf_ref, vt_ref, corr_ref, z_ref, o_ref,
                 qa_ref, m_ref, l_ref, acc_ref, *, seq, tq, tk):
    qi = pl.program_id(2)
    kj = pl.program_id(3)
    nt = (((1,), (1,)), ((), ()))
    hw = 2 * DIFF_HEAD_DIM

    keys_after = kj > qi
    k = k_ref[...]
    kf = kf_ref[...]
    vt = vt_ref[...]

    def step(diag, first, last):
        if first:
            q = q_ref[...]
            lane = lax.broadcasted_iota(jnp.int32, q.shape, 1)
            for c in range(2):
                in_c = (lane >= c * DIFF_HEAD_DIM) & (lane < (c + 1) * DIFF_HEAD_DIM)
                qa_ref[c, :, :hw] = jnp.where(in_c, q, jnp.zeros_like(q))
                qa_ref[c, :, hw:] = qf_ref[...]
        if diag:
            ka_pos = jnp.concatenate([k, kf], axis=1)
            ka_neg = jnp.concatenate([k, -kf], axis=1)
        else:
            ka_any = jnp.concatenate([k, jnp.where(keys_after, -kf, kf)], axis=1)
        chunks = [(kk, c, n) for kk in range(tk // DIFF_KCHUNK) for c in range(2)
                  for n in range(tq // DIFF_QCHUNK)]

        def slices(kk, c, n):
            return (slice(kk * DIFF_KCHUNK, (kk + 1) * DIFF_KCHUNK),
                    slice(n * DIFF_QCHUNK, (n + 1) * DIFF_QCHUNK))

        def scores(kk, c, n):
            ks, cs = slices(kk, c, n)
            qa = qa_ref[c, cs, :]
            if not diag:
                return lax.dot_general(ka_any[ks], qa, nt, preferred_element_type=F32)
            k_lo, q_lo = kk * DIFF_KCHUNK, n * DIFF_QCHUNK
            if k_lo + DIFF_KCHUNK - 1 <= q_lo:
                return lax.dot_general(ka_pos[ks], qa, nt, preferred_element_type=F32)
            if k_lo >= q_lo + DIFF_QCHUNK - 1:
                return lax.dot_general(ka_neg[ks], qa, nt, preferred_element_type=F32)
            off = q_lo - k_lo
            st = lax.dot_general(ka_pos[ks], qa, nt, preferred_element_type=F32)
            return st + corr_ref[:, off:off + DIFF_QCHUNK]

        def softmax(st, kk, c, n):
            _, cs = slices(kk, c, n)
            if first and kk == 0:
                m_new = jnp.max(st, axis=0, keepdims=True)
                p = jnp.exp2(st - m_new)
                l_ref[c, :, cs] = jnp.sum(p, axis=0, keepdims=True)
                m_ref[c, :, cs] = m_new
                return None, p.astype(BF16)
            m_prev = m_ref[c, :, cs]
            m_new = jnp.maximum(m_prev, jnp.max(st, axis=0, keepdims=True))
            alpha = jnp.exp2(m_prev - m_new)
            p = jnp.exp2(st - m_new)
            l_ref[c, :, cs] = alpha * l_ref[c, :, cs] + jnp.sum(p, axis=0, keepdims=True)
            m_ref[c, :, cs] = m_new
            return alpha, p.astype(BF16)

        def accumulate(alpha, p, kk, c, n):
            ks, cs = slices(kk, c, n)
            pv = jnp.dot(vt[:, ks], p, preferred_element_type=F32)
            acc_ref[c, :, cs] = pv if alpha is None else alpha * acc_ref[c, :, cs] + pv

        nc = len(chunks)
        st_q = {0: scores(*chunks[0])}
        if nc > 1:
            st_q[1] = scores(*chunks[1])
        p_q = {0: softmax(st_q.pop(0), *chunks[0])}
        for i in range(nc):
            if i + 2 < nc:
                st_q[i + 2] = scores(*chunks[i + 2])
            if i + 1 < nc:
                p_q[i + 1] = softmax(st_q.pop(i + 1), *chunks[i + 1])
            accumulate(*p_q.pop(i), *chunks[i])

        if last:
            lv = lam_ref[...]
            lambda_init = lv[4:5, 0:1]
            lam = (jnp.exp(jnp.sum(lv[0:1] * lv[1:2], axis=-1, keepdims=True))
                   - jnp.exp(jnp.sum(lv[2:3] * lv[3:4], axis=-1, keepdims=True)) + lambda_init)
            at = acc_ref[0] * (1.0 / l_ref[0]) - acc_ref[1] * (lam / l_ref[1])
            ms = jnp.mean(at * at, axis=0, keepdims=True)
            y = (at * lax.rsqrt(ms + EPS)).T * (sw_ref[...] * (1.0 - lambda_init))
            o_ref[...] = (y * _silu(z_ref[...].astype(F32))).astype(BF16)

    nk = seq // tk
    for first, last in sorted({(j == 0, j == nk - 1) for j in range(nk)}):
        at_pos = (kj == 0 if first else kj != 0) & (kj == nk - 1 if last else kj != nk - 1)
        pl.when(at_pos & (qi != kj))(functools.partial(step, False, first, last))
        pl.when(at_pos & (qi == kj))(functools.partial(step, True, first, last))


def _diff_attention(proj, v_t, lam_vecs, subln_w, batch, seq, tq=2048, tk=2048):
    t = batch * seq
    nq = seq // tq
    nk = seq // tk
    hw = 2 * DIFF_HEAD_DIM
    assert tq == tk and DIFF_KCHUNK % DIFF_QCHUNK == 0, "diagonal handling assumes aligned square tiles"
    qfeat, kfeat = _alibi_features(seq)
    rel = jnp.arange(DIFF_KCHUNK, dtype=F32)[:, None] - jnp.arange(DIFF_KCHUNK, dtype=F32)[None, :]
    corr = -2.0 * jnp.asarray(_alibi_slopes(), F32)[:, None, None] * jnp.maximum(rel, 0.0)[None]
    kern = functools.partial(_diff_kernel, seq=seq, tq=tq, tk=tk)
    return pl.pallas_call(
        kern,
        grid=(batch, DIFF_HEADS, nq, nk),
        in_specs=[
            pl.BlockSpec((5, DIFF_HEAD_DIM), lambda b, h, i, j: (0, 0)),
            pl.BlockSpec((1, hw), lambda b, h, i, j: (0, 0)),
            pl.BlockSpec((tq, hw), lambda b, h, i, j: (b * nq + i, COL_DF_Q // hw + h)),
            pl.BlockSpec((None, tq, LANES), lambda b, h, i, j: (h, i, 0)),
            pl.BlockSpec((tk, hw), lambda b, h, i, j: (b * nk + j, COL_DF_K // hw + h)),
            pl.BlockSpec((None, tk, LANES), lambda b, h, i, j: (h, j, 0)),
            pl.BlockSpec((hw, tk), lambda b, h, i, j: (h, b * nk + j)),
            pl.BlockSpec((None, DIFF_KCHUNK, DIFF_KCHUNK), lambda b, h, i, j: (h, 0, 0)),
            pl.BlockSpec((tq, hw), lambda b, h, i, j: (b * nq + i, (COL_Z + BRANCH_W) // hw + h)),
        ],
        out_specs=pl.BlockSpec((tq, hw), lambda b, h, i, j: (b * nq + i, h)),
        out_shape=jax.ShapeDtypeStruct((t, BRANCH_W), BF16),
        scratch_shapes=[
            pltpu.VMEM((2, tq, 2 * hw), BF16),
            pltpu.VMEM((2, 1, tq), F32),
            pltpu.VMEM((2, 1, tq), F32),
            pltpu.VMEM((2, hw, tq), F32),
        ],
        compiler_params=_params(("parallel", "parallel", "parallel", "arbitrary")),
        name="diff_attention",
    )(lam_vecs, subln_w.reshape(1, hw), proj, qfeat, proj, kfeat, v_t, corr, proj)


POOL_PAD = 16


def _pool_kernel(x_ref, z_ref, w_ref, sc_ref, o_ref, pad_a, pad_b, *, seq):
    t = lax.broadcasted_iota(jnp.int32, (seq, POOL_GROUP_W), 0)
    span = seq + POOL_PAD
    zeros = jnp.zeros((POOL_PAD, POOL_GROUP_W), F32)
    pad_a[span:, :] = zeros
    pad_b[span:, :] = zeros
    for g, win in enumerate(POOL_WINDOWS):
        gs = slice(g * POOL_GROUP_W, (g + 1) * POOL_GROUP_W)
        x = x_ref[:, gs].astype(F32)
        src, dst = pad_a, pad_b
        src[0:POOL_PAD, :] = zeros
        src[POOL_PAD:POOL_PAD + seq, :] = x
        w = 1
        while w < win:
            dst[0:span, :] = src[0:span, :] + src[w:w + span, :]
            src, dst = dst, src
            w *= 2
        start = POOL_PAD - win // 2
        total = src[start:start + seq, :]
        cnt = jnp.minimum(t + (win - win // 2), seq) - jnp.maximum(t - win // 2, 0)
        pooled = total / cnt.astype(F32) - x
        y = jnp.dot(pooled.astype(BF16), w_ref[g], preferred_element_type=F32) * sc_ref[:, gs]
        o_ref[:, gs] = (y * _silu(z_ref[:, gs].astype(F32))).astype(BF16)


def _pool_mixer(proj, pool_w_bf, pool_scale, batch, seq):
    t = batch * seq
    assert POOL_PAD >= max(POOL_WINDOWS)
    return pl.pallas_call(
        functools.partial(_pool_kernel, seq=seq),
        grid=(batch,),
        in_specs=[
            pl.BlockSpec((seq, BRANCH_W), lambda b: (b, COL_POOL // BRANCH_W)),
            pl.BlockSpec((seq, BRANCH_W), lambda b: (b, COL_Z // BRANCH_W + 2)),
            pl.BlockSpec((len(POOL_WINDOWS), POOL_GROUP_W, POOL_GROUP_W), lambda b: (0, 0, 0)),
            pl.BlockSpec((1, BRANCH_W), lambda b: (0, 0)),
        ],
        out_specs=pl.BlockSpec((seq, BRANCH_W), lambda b: (b, 0)),
        out_shape=jax.ShapeDtypeStruct((t, BRANCH_W), BF16),
        scratch_shapes=[pltpu.VMEM((seq + 2 * POOL_PAD, POOL_GROUP_W), F32)] * 2,
        compiler_params=_params(("parallel",)),
        name="pool_mixer",
    )(proj, proj, pool_w_bf, pool_scale.reshape(1, BRANCH_W))


def _sgu_kernel(u_ref, v_ref, z_ref, lw_ref, lb_ref, ws_ref, bs_ref, o_ref, *, tm):
    v = v_ref[...].astype(F32)
    mu = jnp.mean(v, axis=-1, keepdims=True)
    vc = v - mu
    var = jnp.mean(vc * vc, axis=-1, keepdims=True)
    vn = (vc * lax.rsqrt(var + EPS) * lw_ref[...] + lb_ref[...]).astype(BF16)
    for n in range(tm // SGU_CHUNK):
        rs = slice(n * SGU_CHUNK, (n + 1) * SGU_CHUNK)
        for g in range(SGU_GROUPS):
            gs = slice(g * SGU_GROUP_W, (g + 1) * SGU_GROUP_W)
            mixed = jnp.dot(ws_ref[g], vn[rs, gs], preferred_element_type=F32) + bs_ref[g]
            y = u_ref[rs, gs].astype(F32) * mixed
            o_ref[rs, gs] = (y * _silu(z_ref[rs, gs].astype(F32))).astype(BF16)


def _sgu_mixer(proj, ln_w, ln_b, ws_bf, bs_full, t, tm=2048):
    return pl.pallas_call(
        functools.partial(_sgu_kernel, tm=tm),
        grid=(t // tm,),
        in_specs=[
            pl.BlockSpec((tm, BRANCH_W), lambda i: (i, COL_SGU_U // BRANCH_W)),
            pl.BlockSpec((tm, BRANCH_W), lambda i: (i, COL_SGU_V // BRANCH_W)),
            pl.BlockSpec((tm, BRANCH_W), lambda i: (i, COL_Z // BRANCH_W + 3)),
            pl.BlockSpec((1, BRANCH_W), lambda i: (0, 0)),
            pl.BlockSpec((1, BRANCH_W), lambda i: (0, 0)),
            pl.BlockSpec((SGU_GROUPS, SGU_CHUNK, SGU_CHUNK), lambda i: (0, 0, 0)),
            pl.BlockSpec((SGU_GROUPS, SGU_CHUNK, SGU_GROUP_W), lambda i: (0, 0, 0)),
        ],
        out_specs=pl.BlockSpec((tm, BRANCH_W), lambda i: (i, 0)),
        out_shape=jax.ShapeDtypeStruct((t, BRANCH_W), BF16),
        compiler_params=_params(("parallel",)),
        name="sgu_mixer",
    )(proj, proj, proj, ln_w.reshape(1, BRANCH_W), ln_b.reshape(1, BRANCH_W), ws_bf, bs_full)


def _merge_kernel(h_ref, ya_ref, yb_ref, yc_ref, yd_ref, wg_ref, wb_ref, o_ref):
    h = h_ref[...]
    merged = None
    for i, y_ref in enumerate((ya_ref, yb_ref, yc_ref, yd_ref)):
        gate = _sigmoid(jnp.dot(h, wg_ref[i], preferred_element_type=F32))
        term = gate * jnp.dot(y_ref[...], wb_ref[i], preferred_element_type=F32)
        merged = term if merged is None else merged + term
    o_ref[...] = merged.astype(BF16)


def _merge(h, ys, wg_bf, wb_bf, tm=1024, tn=512):
    t = h.shape[0]
    y_spec = pl.BlockSpec((tm, BRANCH_W), lambda i, j: (i, 0))
    return pl.pallas_call(
        _merge_kernel,
        grid=(t // tm, D_MODEL // tn),
        in_specs=[
            pl.BlockSpec((tm, D_MODEL), lambda i, j: (i, 0)),
            y_spec, y_spec, y_spec, y_spec,
            pl.BlockSpec((N_BRANCH, D_MODEL, tn), lambda i, j: (0, 0, j)),
            pl.BlockSpec((N_BRANCH, BRANCH_W, tn), lambda i, j: (0, 0, j)),
        ],
        out_specs=pl.BlockSpec((tm, tn), lambda i, j: (i, j)),
        out_shape=jax.ShapeDtypeStruct((t, D_MODEL), BF16),
        compiler_params=_params(("parallel", "arbitrary")),
        name="merge",
    )(h, *ys, wg_bf, wb_bf)


def _outproj_kernel(x_ref, m_ref, w_ref, fw_ref, o_ref, *, final_norm):
    x = x_ref[...] + jnp.dot(m_ref[...], w_ref[...], preferred_element_type=F32)
    if final_norm:
        ms = jnp.mean(x * x, axis=-1, keepdims=True)
        x = x * lax.rsqrt(ms + EPS) * fw_ref[...]
    o_ref[...] = x


def _outproj(x2, merged, w_out_bf, final_w, final_norm, tm=512):
    t = x2.shape[0]
    return pl.pallas_call(
        functools.partial(_outproj_kernel, final_norm=final_norm),
        grid=(t // tm,),
        in_specs=[
            pl.BlockSpec((tm, D_MODEL), lambda i: (i, 0)),
            pl.BlockSpec((tm, D_MODEL), lambda i: (i, 0)),
            pl.BlockSpec((D_MODEL, D_MODEL), lambda i: (0, 0)),
            pl.BlockSpec((1, D_MODEL), lambda i: (0, 0)),
        ],
        out_specs=pl.BlockSpec((tm, D_MODEL), lambda i: (i, 0)),
        out_shape=jax.ShapeDtypeStruct((t, D_MODEL), F32),
        compiler_params=_params(("parallel",)),
        name="outproj",
    )(x2, merged, w_out_bf, final_w.reshape(1, D_MODEL))


def _column_scale():
    cs = jnp.ones((IN_COLS,), F32)
    cs = cs.at[COL_NA_Q:COL_NA_Q + BRANCH_W].set(NA_HEAD_DIM ** -0.5 * LOG2E)
    cs = cs.at[COL_DF_Q:COL_DF_Q + BRANCH_W].set(DIFF_HEAD_DIM ** -0.5 * LOG2E)
    return cs.reshape(1, IN_COLS)


def kernel(x, norm_w, w_in, na_bias, lam_q1, lam_k1, lam_q2, lam_k2, diff_subln_w, pool_w, pool_scale,
           sgu_ln_w, sgu_ln_b, sgu_ws, sgu_b, w_branch, w_gate, w_out, final_norm_w):
    batch, seq, _ = x.shape
    depth = norm_w.shape[0]
    t = batch * seq
    rows = seq // GRID_W
    x2 = x.reshape(t, D_MODEL)
    col_scale = _column_scale()
    pat_ids, row_sel, col_sel = _na_patterns(rows)
    for l in range(depth):
        h, proj, vt_na, vt_df = _inproj(x2, norm_w[l], w_in[l].astype(BF16), col_scale)
        y_a = _na_attention(proj, vt_na, _na_dense_bias(na_bias[l], row_sel, col_sel), pat_ids, batch, seq)
        lambda_init = 0.8 - 0.6 * math.exp(-0.3 * l)
        lam_vecs = jnp.stack([lam_q1[l], lam_k1[l], lam_q2[l], lam_k2[l],
                              jnp.full((DIFF_HEAD_DIM,), lambda_init)]).astype(F32)
        y_b = _diff_attention(proj, vt_df, lam_vecs, diff_subln_w[l], batch, seq)
        y_c = _pool_mixer(proj, pool_w[l].astype(BF16), pool_scale[l], batch, seq)
        bs_full = jnp.broadcast_to(sgu_b[l][:, :, None], (SGU_GROUPS, SGU_CHUNK, SGU_GROUP_W))
        y_d = _sgu_mixer(proj, sgu_ln_w[l], sgu_ln_b[l], sgu_ws[l].astype(BF16), bs_full, t)
        merged = _merge(h, (y_a, y_b, y_c, y_d), w_gate[l].astype(BF16), w_branch[l].astype(BF16))
        x2 = _outproj(x2, merged, w_out[l].astype(BF16), final_norm_w, l == depth - 1)
    return x2.reshape(batch, seq, D_MODEL)
```

```python
import functools
import math

import jax
import jax.numpy as jnp
import numpy as np
from jax import lax
from jax.experimental import pallas as pl
from jax.experimental.pallas import tpu as pltpu

D_MODEL = 2048
GRID_W = 64
N_BRANCH = 4
BRANCH_W = D_MODEL // N_BRANCH
NA_HEADS = 4
NA_HEAD_DIM = BRANCH_W // NA_HEADS
NA_ROWS_MAX = 8
NA_COLS = 16
DIFF_HEADS = 4
DIFF_HEAD_DIM = BRANCH_W // (2 * DIFF_HEADS)
POOL_WINDOWS = (2, 4, 8, 16)
POOL_GROUP_W = BRANCH_W // len(POOL_WINDOWS)
SGU_CHUNK = 128
SGU_GROUPS = 4
SGU_GROUP_W = BRANCH_W // SGU_GROUPS
IN_COLS = 13 * BRANCH_W
EPS = 1e-6

COL_NA_Q = 0
COL_NA_K = BRANCH_W
COL_NA_V = 2 * BRANCH_W
COL_DF_Q = 3 * BRANCH_W
COL_DF_K = 4 * BRANCH_W
COL_DF_V = 5 * BRANCH_W
COL_POOL = 6 * BRANCH_W
COL_SGU_U = 7 * BRANCH_W
COL_SGU_V = 8 * BRANCH_W
COL_Z = 9 * BRANCH_W

LANES = 128
VMEM_LIMIT = 56 * 1024 * 1024
MASK_VALUE = -1e30
LOG2E = math.log2(math.e)

BF16 = jnp.bfloat16
F32 = jnp.float32


def _params(sem, vmem=VMEM_LIMIT):
    return pltpu.CompilerParams(dimension_semantics=sem, vmem_limit_bytes=vmem)


def _silu(z):
    return z * (1.0 / (1.0 + jnp.exp(-z)))


def _sigmoid(z):
    return 1.0 / (1.0 + jnp.exp(-z))


def _inproj_kernel(x_ref, nw_ref, w_ref, cs_ref, h_ref, p_ref, vta_ref, vtd_ref, *, vt_places):
    @pl.when(pl.program_id(1) == 0)
    def _():
        x = x_ref[...]
        ms = jnp.mean(x * x, axis=-1, keepdims=True)
        h_ref[...] = (x * lax.rsqrt(ms + EPS) * nw_ref[...]).astype(BF16)

    acc = jnp.dot(h_ref[...], w_ref[...], preferred_element_type=F32) * cs_ref[...]
    p_ref[...] = acc.astype(BF16)

    for vt_ref, (tile, off) in zip((vta_ref, vtd_ref), vt_places):
        @pl.when(pl.program_id(1) == tile)
        def _(vt_ref=vt_ref, off=off):
            vt_ref[...] = acc[:, off:off + BRANCH_W].T.astype(BF16)


def _inproj(x2, norm_w, w_in_bf, col_scale, tm=512, tn=IN_COLS // 2):
    t = x2.shape[0]
    vt_places = tuple(divmod(col, tn) for col in (COL_NA_V, COL_DF_V))
    for _, off in vt_places:
        assert off + BRANCH_W <= tn and off % LANES == 0, "value columns must sit inside one column tile"
    vt_spec = pl.BlockSpec((BRANCH_W, tm), lambda i, j: (0, i))
    vt_shape = jax.ShapeDtypeStruct((BRANCH_W, t), BF16)
    return pl.pallas_call(
        functools.partial(_inproj_kernel, vt_places=vt_places),
        grid=(t // tm, IN_COLS // tn),
        in_specs=[
            pl.BlockSpec((tm, D_MODEL), lambda i, j: (i, 0)),
            pl.BlockSpec((1, D_MODEL), lambda i, j: (0, 0)),
            pl.BlockSpec((D_MODEL, tn), lambda i, j: (0, j)),
            pl.BlockSpec((1, tn), lambda i, j: (0, j)),
        ],
        out_specs=[
            pl.BlockSpec((tm, D_MODEL), lambda i, j: (i, 0)),
            pl.BlockSpec((tm, tn), lambda i, j: (i, j)),
            vt_spec, vt_spec,
        ],
        out_shape=[
            jax.ShapeDtypeStruct((t, D_MODEL), BF16),
            jax.ShapeDtypeStruct((t, IN_COLS), BF16),
            vt_shape, vt_shape,
        ],
        compiler_params=_params(("parallel", "arbitrary")),
        name="inproj",
    )(x2, norm_w.reshape(1, D_MODEL), w_in_bf, col_scale)


NA_QROWS = 8
NA_KROWS = 16
NA_TQ = NA_QROWS * GRID_W
NA_TK = NA_KROWS * GRID_W
NA_KSPLIT = 4
NA_KBLK = NA_TK // NA_KSPLIT


def _na_window_start(rb, rows):
    lo = rb * NA_QROWS - NA_ROWS_MAX // 2
    return jnp.clip(lo, 0, rows - NA_KROWS)


def _na_patterns(rows):
    nrb = rows // NA_QROWS
    kr_win = min(NA_ROWS_MAX, rows)
    n_roff = 2 * NA_ROWS_MAX - 1
    n_coff = 2 * NA_COLS - 1
    pats, pat_ids = [], []
    for rb in range(nrb):
        r = rb * NA_QROWS + np.arange(NA_QROWS)
        start = int(np.clip(rb * NA_QROWS - NA_ROWS_MAX // 2, 0, rows - NA_KROWS))
        kr = start + np.arange(NA_KROWS)
        rs = np.clip(r - kr_win // 2, 0, rows - kr_win)
        ok = (kr[None, :] >= rs[:, None]) & (kr[None, :] < rs[:, None] + kr_win)
        off = kr[None, :] - r[:, None] + (NA_ROWS_MAX - 1)
        sel = np.zeros((NA_QROWS, NA_KROWS, n_roff + 1), np.float32)
        for i in range(NA_QROWS):
            for j in range(NA_KROWS):
                sel[i, j, off[i, j] if ok[i, j] else n_roff] = 1.0
        for pid, p in enumerate(pats):
            if np.array_equal(p, sel):
                pat_ids.append(pid)
                break
        else:
            pat_ids.append(len(pats))
            pats.append(sel)
    c = np.arange(GRID_W)
    cs = np.clip(c - NA_COLS // 2, 0, GRID_W - NA_COLS)
    col_sel = np.zeros((GRID_W, GRID_W, n_coff), np.float32)
    for qc in range(GRID_W):
        for kc in range(cs[qc], cs[qc] + NA_COLS):
            col_sel[qc, kc, kc - qc + (NA_COLS - 1)] = 1.0
    return pat_ids, np.stack(pats), col_sel


def _na_dense_bias(rel_bias, row_sel, col_sel):
    npat = row_sel.shape[0]
    n_sel = row_sel.shape[-1]
    assert 2 * GRID_W == LANES
    cols = jnp.einsum("hab,qkb->hakq", rel_bias.astype(F32), col_sel, precision=lax.Precision.HIGHEST) * LOG2E
    cols = jnp.where((col_sel.sum(-1) > 0).T[None, None], cols, MASK_VALUE)
    cols = jnp.concatenate([cols, jnp.full((NA_HEADS, 1, GRID_W, GRID_W), MASK_VALUE, F32)], axis=1)
    cols2 = jnp.concatenate([cols, cols], axis=-1)
    tile_ids = jnp.asarray(np.argmax(row_sel, axis=-1).reshape(-1), jnp.int32)

    def body(ids_ref, cols_ref, o_ref):
        base = pl.program_id(0) * (NA_QROWS * NA_KROWS)
        left = lax.broadcasted_iota(jnp.int32, (GRID_W, LANES), 1) < GRID_W
        for j in range(NA_KROWS):
            for ii in range(NA_QROWS // 2):
                even = cols_ref[ids_ref[base + 2 * ii * NA_KROWS + j]]
                odd = cols_ref[ids_ref[base + (2 * ii + 1) * NA_KROWS + j]]
                o_ref[j * GRID_W:(j + 1) * GRID_W, ii * LANES:(ii + 1) * LANES] = jnp.where(left, even, odd)

    return pl.pallas_call(
        body,
        grid_spec=pltpu.PrefetchScalarGridSpec(
            num_scalar_prefetch=1,
            grid=(npat, NA_HEADS),
            in_specs=[pl.BlockSpec((None, n_sel, GRID_W, LANES), lambda p, h, ids: (h, 0, 0, 0))],
            out_specs=pl.BlockSpec((None, None, NA_TK, NA_TQ), lambda p, h, ids: (p, h, 0, 0)),
        ),
        out_shape=jax.ShapeDtypeStruct((npat, NA_HEADS, NA_TK, NA_TQ), F32),
        compiler_params=_params(("parallel", "parallel")),
        name="na_bias_expand",
    )(tile_ids, cols2)


def _na_kernel(q_ref, k0, k1, k2, k3, v0, v1, v2, v3, z_ref, b_ref, o_ref):
    nt = (((1,), (1,)), ((), ()))
    k_all = jnp.concatenate([r[...] for r in (k0, k1, k2, k3)], axis=0)
    vt_all = jnp.concatenate([r[...] for r in (v0, v1, v2, v3)], axis=1)

    def head(h):
        return slice(h * NA_HEAD_DIM, (h + 1) * NA_HEAD_DIM)

    def scores(h):
        st = lax.dot_general(k_all[:, head(h)], q_ref[:, head(h)], nt, preferred_element_type=F32)
        return st + b_ref[h]

    def softmax(st):
        p = jnp.exp2(st - jnp.max(st, axis=0, keepdims=True))
        return jnp.sum(p, axis=0, keepdims=True), p.astype(BF16)

    def output(l, p, h):
        ot = jnp.dot(vt_all[head(h)], p, preferred_element_type=F32) / l
        o_ref[:, head(h)] = (ot.T * _silu(z_ref[:, head(h)].astype(F32))).astype(BF16)

    st_next = scores(0)
    pending = None
    for h in range(NA_HEADS):
        st = st_next
        if h + 1 < NA_HEADS:
            st_next = scores(h + 1)
        lp = softmax(st)
        if pending is not None:
            output(*pending)
        pending = (*lp, h)
    output(*pending)


def _na_attention(proj, v_t, dense_bias, pat_ids, batch, seq):
    rows = seq // GRID_W
    nrb = rows // NA_QROWS
    t = batch * seq
    qblk_per_seq = seq // NA_TQ
    kblk_per_seq = seq // NA_KBLK
    kblk_per_row = NA_KBLK // GRID_W

    def q_map(col):
        return lambda rb, b: (b * qblk_per_seq + rb, col)

    def k_map(j):
        def f(rb, b):
            start = _na_window_start(rb, rows) // kblk_per_row
            return (b * kblk_per_seq + start + j, COL_NA_K // BRANCH_W)
        return f

    def vt_map(j):
        return lambda rb, b: (0, k_map(j)(rb, b)[0])

    def bias_map(rb, b):
        pid = 0
        for r, p in enumerate(pat_ids):
            pid = pid + jnp.where(rb == r, p, 0)
        return (pid, 0, 0, 0)

    in_specs = [pl.BlockSpec((NA_TQ, BRANCH_W), q_map(COL_NA_Q // BRANCH_W))]
    in_specs += [pl.BlockSpec((NA_KBLK, BRANCH_W), k_map(j)) for j in range(NA_KSPLIT)]
    in_specs += [pl.BlockSpec((BRANCH_W, NA_KBLK), vt_map(j)) for j in range(NA_KSPLIT)]
    in_specs += [pl.BlockSpec((NA_TQ, BRANCH_W), q_map(COL_Z // BRANCH_W + 0))]
    in_specs += [pl.BlockSpec((None, NA_HEADS, NA_TK, NA_TQ), bias_map)]
    return pl.pallas_call(
        _na_kernel,
        grid=(nrb, batch),
        in_specs=in_specs,
        out_specs=pl.BlockSpec((NA_TQ, BRANCH_W), lambda rb, b: (b * qblk_per_seq + rb, 0)),
        out_shape=jax.ShapeDtypeStruct((t, BRANCH_W), BF16),
        compiler_params=_params(("parallel", "parallel")),
        name="na_attention",
    )(*([proj] * 5), *([v_t] * 4), proj, dense_bias)


DIFF_QCHUNK = 512
DIFF_KCHUNK = 1024


def _alibi_slopes():
    return [2.0 ** (-8.0 * (h + 1) / DIFF_HEADS) * LOG2E for h in range(DIFF_HEADS)]


def _split3(x):
    def trunc(v):
        return (v.view(np.uint32) & np.uint32(0xFFFF0000)).view(np.float32)
    x1 = trunc(x)
    r1 = x - x1
    x2 = trunc(r1)
    return x1, x2, r1 - x2


def _alibi_features(seq):
    pos = np.arange(seq, dtype=np.float32)
    slopes = np.asarray(_alibi_slopes(), np.float32)[:, None]
    u = _split3(-slopes * pos[None, :])
    w = _split3(slopes * pos[None, :])
    one = np.ones((DIFF_HEADS, seq), np.float32)
    zero = np.zeros((DIFF_HEADS, seq, LANES - 6), np.float32)
    qfeat = np.concatenate([np.stack(list(u) + [one] * 3, axis=-1), zero], axis=-1)
    kfeat = np.concatenate([np.stack([one] * 3 + list(w), axis=-1), zero], axis=-1)
    return jnp.asarray(qfeat).astype(BF16), jnp.asarray(kfeat).astype(BF16)


def _diff_kernel(lam_ref, sw_ref, q_ref, qf_ref, k_ref, kf_ref, vt_ref, corr_ref, z_ref, o_ref,
                 qa_ref, ka_ref, m_ref, l_ref, acc_ref, *, tq, tk, lambda_init):
    qi = pl.program_id(2)
    kj = pl.program_id(3)
    nt = (((1,), (1,)), ((), ()))
    hw = 2 * DIFF_HEAD_DIM

    @pl.when(kj == 0)
    def _():
        m_ref[...] = jnp.full(m_ref.shape, MASK_VALUE, F32)
        l_ref[...] = jnp.zeros(l_ref.shape, F32)
        acc_ref[...] = jnp.zeros(acc_ref.shape, F32)
        q = q_ref[...]
        lane = lax.broadcasted_iota(jnp.int32, q.shape, 1)
        for c in range(2):
            in_c = (lane >= c * DIFF_HEAD_DIM) & (lane < (c + 1) * DIFF_HEAD_DIM)
            qa_ref[c, :, :hw] = jnp.where(in_c, q, jnp.zeros_like(q))
            qa_ref[c, :, hw:] = qf_ref[...]

    rel = kj - qi
    on_diag = rel == 0
    k = k_ref[...]
    kf = kf_ref[...]
    vt = vt_ref[...]
    ka_ref[0, :, :hw] = k
    ka_ref[1, :, :hw] = k
    ka_ref[0, :, hw:] = kf
    ka_ref[1, :, hw:] = -kf

    def step():
        chunks = [(kk, c, n) for kk in range(tk // DIFF_KCHUNK) for c in range(2)
                  for n in range(tq // DIFF_QCHUNK)]

        def slices(kk, c, n):
            return (slice(kk * DIFF_KCHUNK, (kk + 1) * DIFF_KCHUNK),
                    slice(n * DIFF_QCHUNK, (n + 1) * DIFF_QCHUNK))

        def scores(kk, c, n):
            ks, cs = slices(kk, c, n)
            k_lo, q_lo = kk * DIFF_KCHUNK, n * DIFF_QCHUNK
            crosses = False
            if k_lo + DIFF_KCHUNK - 1 <= q_lo:
                diag_sign = 0
            elif k_lo >= q_lo + DIFF_QCHUNK - 1:
                diag_sign = 1
            else:
                diag_sign, crosses = 0, True
            sign = jnp.where(on_diag, diag_sign, jnp.where(rel > 0, 1, 0))
            st = lax.dot_general(ka_ref[sign, ks, :], qa_ref[c, cs, :], nt, preferred_element_type=F32)
            if crosses:
                off = q_lo - k_lo
                st = st + jnp.where(on_diag, corr_ref[:, off:off + DIFF_QCHUNK], 0.0)
            return st

        def softmax(st, kk, c, n):
            _, cs = slices(kk, c, n)
            m_prev = m_ref[c, :, cs]
            m_new = jnp.maximum(m_prev, jnp.max(st, axis=0, keepdims=True))
            alpha = jnp.exp2(m_prev - m_new)
            p = jnp.exp2(st - m_new)
            l_ref[c, :, cs] = alpha * l_ref[c, :, cs] + jnp.sum(p, axis=0, keepdims=True)
            m_ref[c, :, cs] = m_new
            return alpha, p.astype(BF16)

        def accumulate(alpha, p, kk, c, n):
            ks, cs = slices(kk, c, n)
            acc_ref[c, :, cs] = alpha * acc_ref[c, :, cs] + jnp.dot(vt[:, ks], p, preferred_element_type=F32)

        nc = len(chunks)
        st_q = {0: scores(*chunks[0])}
        if nc > 1:
            st_q[1] = scores(*chunks[1])
        p_q = {0: softmax(st_q.pop(0), *chunks[0])}
        for i in range(nc):
            if i + 2 < nc:
                st_q[i + 2] = scores(*chunks[i + 2])
            if i + 1 < nc:
                p_q[i + 1] = softmax(st_q.pop(i + 1), *chunks[i + 1])
            accumulate(*p_q.pop(i), *chunks[i])

    step()

    @pl.when(kj == pl.num_programs(3) - 1)
    def _():
        lv = lam_ref[...]
        lam = (jnp.exp(jnp.sum(lv[0:1] * lv[1:2], axis=-1, keepdims=True))
               - jnp.exp(jnp.sum(lv[2:3] * lv[3:4], axis=-1, keepdims=True)) + lambda_init)
        at = acc_ref[0] / l_ref[0] - lam * (acc_ref[1] / l_ref[1])
        ms = jnp.mean(at * at, axis=0, keepdims=True)
        y = (at * lax.rsqrt(ms + EPS)).T * sw_ref[...] * (1.0 - lambda_init)
        o_ref[...] = (y * _silu(z_ref[...].astype(F32))).astype(BF16)


def _diff_attention(proj, v_t, lam_vecs, subln_w, batch, seq, lambda_init, tq=2048, tk=2048):
    t = batch * seq
    nq = seq // tq
    nk = seq // tk
    hw = 2 * DIFF_HEAD_DIM
    assert tq == tk and DIFF_KCHUNK % DIFF_QCHUNK == 0, "diagonal handling assumes aligned square tiles"
    qfeat, kfeat = _alibi_features(seq)
    rel = jnp.arange(DIFF_KCHUNK, dtype=F32)[:, None] - jnp.arange(DIFF_KCHUNK, dtype=F32)[None, :]
    corr = -2.0 * jnp.asarray(_alibi_slopes(), F32)[:, None, None] * jnp.maximum(rel, 0.0)[None]
    kern = functools.partial(_diff_kernel, tq=tq, tk=tk, lambda_init=lambda_init)
    return pl.pallas_call(
        kern,
        grid=(batch, DIFF_HEADS, nq, nk),
        in_specs=[
            pl.BlockSpec((4, DIFF_HEAD_DIM), lambda b, h, i, j: (0, 0)),
            pl.BlockSpec((1, hw), lambda b, h, i, j: (0, 0)),
            pl.BlockSpec((tq, hw), lambda b, h, i, j: (b * nq + i, COL_DF_Q // hw + h)),
            pl.BlockSpec((None, tq, LANES), lambda b, h, i, j: (h, i, 0)),
            pl.BlockSpec((tk, hw), lambda b, h, i, j: (b * nk + j, COL_DF_K // hw + h)),
            pl.BlockSpec((None, tk, LANES), lambda b, h, i, j: (h, j, 0)),
            pl.BlockSpec((hw, tk), lambda b, h, i, j: (h, b * nk + j)),
            pl.BlockSpec((None, DIFF_KCHUNK, DIFF_KCHUNK), lambda b, h, i, j: (h, 0, 0)),
            pl.BlockSpec((tq, hw), lambda b, h, i, j: (b * nq + i, (COL_Z + BRANCH_W) // hw + h)),
        ],
        out_specs=pl.BlockSpec((tq, hw), lambda b, h, i, j: (b * nq + i, h)),
        out_shape=jax.ShapeDtypeStruct((t, BRANCH_W), BF16),
        scratch_shapes=[
            pltpu.VMEM((2, tq, 2 * hw), BF16),
            pltpu.VMEM((2, tk, 2 * hw), BF16),
            pltpu.VMEM((2, 1, tq), F32),
            pltpu.VMEM((2, 1, tq), F32),
            pltpu.VMEM((2, hw, tq), F32),
        ],
        compiler_params=_params(("parallel", "parallel", "parallel", "arbitrary")),
        name="diff_attention",
    )(lam_vecs, subln_w.reshape(1, hw), proj, qfeat, proj, kfeat, v_t, corr, proj)


POOL_PAD = 8


def _pool_kernel(x_ref, z_ref, w_ref, sc_ref, o_ref, pad_ref, *, seq):
    t = lax.broadcasted_iota(jnp.int32, (seq, POOL_GROUP_W), 0)
    zeros = jnp.zeros((POOL_PAD, POOL_GROUP_W), F32)
    pad_ref[0:POOL_PAD, :] = zeros
    pad_ref[POOL_PAD + seq:, :] = zeros
    for g, win in enumerate(POOL_WINDOWS):
        gs = slice(g * POOL_GROUP_W, (g + 1) * POOL_GROUP_W)
        x = x_ref[:, gs].astype(F32)
        pad_ref[POOL_PAD:POOL_PAD + seq, :] = x
        half = win // 2
        total = x
        for d in range(-half, win - half):
            if d != 0:
                total = total + pad_ref[POOL_PAD + d:POOL_PAD + d + seq, :]
        cnt = jnp.minimum(t + (win - half), seq) - jnp.maximum(t - half, 0)
        pooled = total / cnt.astype(F32) - x
        y = jnp.dot(pooled.astype(BF16), w_ref[g], preferred_element_type=F32) * sc_ref[:, gs]
        o_ref[:, gs] = (y * _silu(z_ref[:, gs].astype(F32))).astype(BF16)


def _pool_mixer(proj, pool_w_bf, pool_scale, batch, seq):
    t = batch * seq
    assert POOL_PAD >= max(POOL_WINDOWS) // 2
    return pl.pallas_call(
        functools.partial(_pool_kernel, seq=seq),
        grid=(batch,),
        in_specs=[
            pl.BlockSpec((seq, BRANCH_W), lambda b: (b, COL_POOL // BRANCH_W)),
            pl.BlockSpec((seq, BRANCH_W), lambda b: (b, COL_Z // BRANCH_W + 2)),
            pl.BlockSpec((len(POOL_WINDOWS), POOL_GROUP_W, POOL_GROUP_W), lambda b: (0, 0, 0)),
            pl.BlockSpec((1, BRANCH_W), lambda b: (0, 0)),
        ],
        out_specs=pl.BlockSpec((seq, BRANCH_W), lambda b: (b, 0)),
        out_shape=jax.ShapeDtypeStruct((t, BRANCH_W), BF16),
        scratch_shapes=[pltpu.VMEM((seq + 2 * POOL_PAD, POOL_GROUP_W), F32)],
        compiler_params=_params(("parallel",)),
        name="pool_mixer",
    )(proj, proj, pool_w_bf, pool_scale.reshape(1, BRANCH_W))


def _sgu_kernel(u_ref, v_ref, z_ref, lw_ref, lb_ref, ws_ref, bs_ref, o_ref, *, tm):
    v = v_ref[...].astype(F32)
    mu = jnp.mean(v, axis=-1, keepdims=True)
    vc = v - mu
    var = jnp.mean(vc * vc, axis=-1, keepdims=True)
    vn = (vc * lax.rsqrt(var + EPS) * lw_ref[...] + lb_ref[...]).astype(BF16)
    for n in range(tm // SGU_CHUNK):
        rs = slice(n * SGU_CHUNK, (n + 1) * SGU_CHUNK)
        for g in range(SGU_GROUPS):
            gs = slice(g * SGU_GROUP_W, (g + 1) * SGU_GROUP_W)
            mixed = jnp.dot(ws_ref[g], vn[rs, gs], preferred_element_type=F32) + bs_ref[g]
            y = u_ref[rs, gs].astype(F32) * mixed
            o_ref[rs, gs] = (y * _silu(z_ref[rs, gs].astype(F32))).astype(BF16)


def _sgu_mixer(proj, ln_w, ln_b, ws_bf, bs_full, t, tm=2048):
    return pl.pallas_call(
        functools.partial(_sgu_kernel, tm=tm),
        grid=(t // tm,),
        in_specs=[
            pl.BlockSpec((tm, BRANCH_W), lambda i: (i, COL_SGU_U // BRANCH_W)),
            pl.BlockSpec((tm, BRANCH_W), lambda i: (i, COL_SGU_V // BRANCH_W)),
            pl.BlockSpec((tm, BRANCH_W), lambda i: (i, COL_Z // BRANCH_W + 3)),
            pl.BlockSpec((1, BRANCH_W), lambda i: (0, 0)),
            pl.BlockSpec((1, BRANCH_W), lambda i: (0, 0)),
            pl.BlockSpec((SGU_GROUPS, SGU_CHUNK, SGU_CHUNK), lambda i: (0, 0, 0)),
            pl.BlockSpec((SGU_GROUPS, SGU_CHUNK, SGU_GROUP_W), lambda i: (0, 0, 0)),
        ],
        out_specs=pl.BlockSpec((tm, BRANCH_W), lambda i: (i, 0)),
        out_shape=jax.ShapeDtypeStruct((t, BRANCH_W), BF16),
        compiler_params=_params(("parallel",)),
        name="sgu_mixer",
    )(proj, proj, proj, ln_w.reshape(1, BRANCH_W), ln_b.reshape(1, BRANCH_W), ws_bf, bs_full)


def _merge_kernel(h_ref, ya_ref, yb_ref, yc_ref, yd_ref, wg_ref, wb_ref, o_ref):
    h = h_ref[...]
    merged = None
    for i, y_ref in enumerate((ya_ref, yb_ref, yc_ref, yd_ref)):
        gate = _sigmoid(jnp.dot(h, wg_ref[i], preferred_element_type=F32))
        term = gate * jnp.dot(y_ref[...], wb_ref[i], preferred_element_type=F32)
        merged = term if merged is None else merged + term
    o_ref[...] = merged.astype(BF16)


def _merge(h, ys, wg_bf, wb_bf, tm=1024, tn=512):
    t = h.shape[0]
    y_spec = pl.BlockSpec((tm, BRANCH_W), lambda i, j: (i, 0))
    return pl.pallas_call(
        _merge_kernel,
        grid=(t // tm, D_MODEL // tn),
        in_specs=[
            pl.BlockSpec((tm, D_MODEL), lambda i, j: (i, 0)),
            y_spec, y_spec, y_spec, y_spec,
            pl.BlockSpec((N_BRANCH, D_MODEL, tn), lambda i, j: (0, 0, j)),
            pl.BlockSpec((N_BRANCH, BRANCH_W, tn), lambda i, j: (0, 0, j)),
        ],
        out_specs=pl.BlockSpec((tm, tn), lambda i, j: (i, j)),
        out_shape=jax.ShapeDtypeStruct((t, D_MODEL), BF16),
        compiler_params=_params(("parallel", "arbitrary")),
        name="merge",
    )(h, *ys, wg_bf, wb_bf)


def _outproj_kernel(x_ref, m_ref, w_ref, fw_ref, o_ref, *, final_norm):
    x = x_ref[...] + jnp.dot(m_ref[...], w_ref[...], preferred_element_type=F32)
    if final_norm:
        ms = jnp.mean(x * x, axis=-1, keepdims=True)
        x = x * lax.rsqrt(ms + EPS) * fw_ref[...]
    o_ref[...] = x


def _outproj(x2, merged, w_out_bf, final_w, final_norm, tm=512):
    t = x2.shape[0]
    return pl.pallas_call(
        functools.partial(_outproj_kernel, final_norm=final_norm),
        grid=(t // tm,),
        in_specs=[
            pl.BlockSpec((tm, D_MODEL), lambda i: (i, 0)),
            pl.BlockSpec((tm, D_MODEL), lambda i: (i, 0)),
            pl.BlockSpec((D_MODEL, D_MODEL), lambda i: (0, 0)),
            pl.BlockSpec((1, D_MODEL), lambda i: (0, 0)),
        ],
        out_specs=pl.BlockSpec((tm, D_MODEL), lambda i: (i, 0)),
        out_shape=jax.ShapeDtypeStruct((t, D_MODEL), F32),
        compiler_params=_params(("parallel",)),
        name="outproj",
    )(x2, merged, w_out_bf, final_w.reshape(1, D_MODEL))


def _column_scale():
    cs = jnp.ones((IN_COLS,), F32)
    cs = cs.at[COL_NA_Q:COL_NA_Q + BRANCH_W].set(NA_HEAD_DIM ** -0.5 * LOG2E)
    cs = cs.at[COL_DF_Q:COL_DF_Q + BRANCH_W].set(DIFF_HEAD_DIM ** -0.5 * LOG2E)
    return cs.reshape(1, IN_COLS)


def kernel(x, norm_w, w_in, na_bias, lam_q1, lam_k1, lam_q2, lam_k2, diff_subln_w, pool_w, pool_scale,
           sgu_ln_w, sgu_ln_b, sgu_ws, sgu_b, w_branch, w_gate, w_out, final_norm_w):
    batch, seq, _ = x.shape
    depth = norm_w.shape[0]
    t = batch * seq
    rows = seq // GRID_W
    x2 = x.reshape(t, D_MODEL)
    col_scale = _column_scale()
    pat_ids, row_sel, col_sel = _na_patterns(rows)
    for l in range(depth):
        h, proj, vt_na, vt_df = _inproj(x2, norm_w[l], w_in[l].astype(BF16), col_scale)
        y_a = _na_attention(proj, vt_na, _na_dense_bias(na_bias[l], row_sel, col_sel), pat_ids, batch, seq)
        lambda_init = 0.8 - 0.6 * math.exp(-0.3 * l)
        lam_vecs = jnp.stack([lam_q1[l], lam_k1[l], lam_q2[l], lam_k2[l]]).astype(F32)
        y_b = _diff_attention(proj, vt_df, lam_vecs, diff_subln_w[l], batch, seq, lambda_init)
        y_c = _pool_mixer(proj, pool_w[l].astype(BF16), pool_scale[l], batch, seq)
        bs_full = jnp.broadcast_to(sgu_b[l][:, :, None], (SGU_GROUPS, SGU_CHUNK, SGU_GROUP_W))
        y_d = _sgu_mixer(proj, sgu_ln_w[l], sgu_ln_b[l], sgu_ws[l].astype(BF16), bs_full, t)
        merged = _merge(h, (y_a, y_b, y_c, y_d), w_gate[l].astype(BF16), w_branch[l].astype(BF16))
        x2 = _outproj(x2, merged, w_out[l].astype(BF16), final_norm_w, l == depth - 1)
    return x2.reshape(batch, seq, D_MODEL)
```
